```python
import jax, jax.numpy as jnp
from jax import lax
import numpy as np

D_MODEL = 1024
BATCH = 16
SEQ = 2048
DEPTH = 2

CHUNK = 64
N_MIXERS = 2
EXPAND = 2
D_INNER = EXPAND * D_MODEL
SG_BLOCK = 128
SG_GROUPS = 8
SG_GROUP_DIM = D_INNER // SG_GROUPS
HG_HEADS = 16
HG_HEAD_DIM = D_INNER // HG_HEADS
N_LAYERS_A = (DEPTH + 1) // 2
N_LAYERS_B = DEPTH // 2
EPS = 1e-6

kernel_name = "hybrid_gmlp_hgrn2_adaln_trunk"


def rms_norm(x, gain):
    xf = x.astype(jnp.float32)
    y = xf * lax.rsqrt(jnp.mean(xf * xf, axis=-1, keepdims=True) + EPS)
    return (y * gain.astype(jnp.float32)).astype(x.dtype)


def layer_norm(x, gain, bias):
    xf = x.astype(jnp.float32)
    mu = jnp.mean(xf, axis=-1, keepdims=True)
    var = jnp.mean(jnp.square(xf - mu), axis=-1, keepdims=True)
    y = (xf - mu) * lax.rsqrt(var + EPS) * gain.astype(jnp.float32) + bias.astype(jnp.float32)
    return y.astype(x.dtype)


def spatial_gating_mixer(h, w_in, ln_gain, ln_bias, w_s, b_s, w_out):
    bsz, seq, _ = h.shape
    proj = h @ w_in
    uv, g = proj[..., : 2 * D_INNER], proj[..., 2 * D_INNER:]
    uv = jax.nn.gelu(uv)
    u, v = uv[..., :D_INNER], uv[..., D_INNER:]
    v = layer_norm(v, ln_gain, ln_bias)
    nb = seq // SG_BLOCK
    v = v.reshape(bsz, nb, SG_BLOCK, SG_GROUPS, SG_GROUP_DIM)
    pos = jnp.arange(SG_BLOCK)
    mask = (pos[None, :] // CHUNK) <= (pos[:, None] // CHUNK)
    ws = jnp.where(mask[None], w_s, jnp.zeros((), w_s.dtype))
    s = jnp.einsum('gts,bnsgd->bntgd', ws, v) + b_s.T[None, None, :, :, None]
    s = s.reshape(bsz, seq, D_INNER)
    y = u * s * jax.nn.silu(g)
    return y @ w_out


def hgrn2_mixer(h, w_in, lower_bound, gn_gain, w_out):
    bsz, seq, _ = h.shape
    f32 = jnp.float32
    proj = h @ w_in
    q = proj[..., :D_INNER]
    f = proj[..., D_INNER: 2 * D_INNER]
    i = proj[..., 2 * D_INNER: 3 * D_INNER]
    g = proj[..., 3 * D_INNER:]
    q = jax.nn.silu(q.astype(f32))
    lb = lower_bound.astype(f32)
    f = lb + (1.0 - lb) * jax.nn.sigmoid(f.astype(f32))
    k = 1.0 - f
    log_f = jnp.log(f)
    nc = seq // CHUNK

    def heads(z):
        return z.reshape(bsz, nc, CHUNK, HG_HEADS, HG_HEAD_DIM).transpose(0, 3, 1, 2, 4)

    q, k, v, log_f = heads(q), heads(k), heads(i.astype(f32)), heads(log_f)
    a = jnp.cumsum(log_f, axis=3)
    a_ref = a[:, :, :, CHUNK // 2 - 1: CHUNK // 2, :]
    a_last = a[:, :, :, CHUNK - 1:, :]
    q_in = q * jnp.exp(a - a_ref)
    k_in = k * jnp.exp(a_ref - a)
    scores = jnp.einsum('bhnck,bhnsk->bhncs', q_in, k_in)
    causal = jnp.tril(jnp.ones((CHUNK, CHUNK), dtype=bool))
    scores = jnp.where(causal, scores, jnp.zeros((), f32))
    o_intra = jnp.einsum('bhncs,bhnsv->bhncv', scores, v)
    q_out = q * jnp.exp(a)
    k_out = k * jnp.exp(a_last - a)
    decay = jnp.exp(a_last[:, :, :, 0, :])

    def step(state, xs):
        q_c, k_c, v_c, d_c = xs
        o_c = jnp.einsum('bhck,bhkv->bhcv', q_c, state)
        state = d_c[..., None] * state + jnp.einsum('bhck,bhcv->bhkv', k_c, v_c)
        return state, o_c

    xs = (jnp.moveaxis(q_out, 2, 0), jnp.moveaxis(k_out, 2, 0),
          jnp.moveaxis(v, 2, 0), jnp.moveaxis(decay, 2, 0))
    init = jnp.zeros((bsz, HG_HEADS, HG_HEAD_DIM, HG_HEAD_DIM), f32)
    _, o_inter = lax.scan(step, init, xs)
    o = o_intra + jnp.moveaxis(o_inter, 0, 2)
    o = o.transpose(0, 2, 3, 1, 4)
    o = rms_norm(o, gn_gain)
    o = o.reshape(bsz, seq, D_INNER).astype(h.dtype)
    return (o * jax.nn.silu(g)) @ w_out


def setup_inputs(seed: int = 0) -> dict:
    key = jax.random.key(seed)
    ks = jax.random.split(key, 20)
    nrm = jax.random.normal
    f32 = jnp.float32
    D, DI = D_MODEL, D_INNER
    return {
        "x": nrm(ks[0], (BATCH, SEQ, D), f32),
        "c": nrm(ks[1], (BATCH, D), f32),
        "norm_gain": 1.0 + 0.02 * nrm(ks[2], (DEPTH, D), f32),
        "w_ada": 0.5 * D ** -0.5 * nrm(ks[3], (DEPTH, D, 3 * D), f32),
        "b_ada": 0.02 * nrm(ks[4], (DEPTH, 3 * D), f32),
        "a_w_in": D ** -0.5 * nrm(ks[5], (N_LAYERS_A, D, 3 * DI), f32),
        "a_ln_gain": 1.0 + 0.02 * nrm(ks[6], (N_LAYERS_A, DI), f32),
        "a_ln_bias": 0.02 * nrm(ks[7], (N_LAYERS_A, DI), f32),
        "a_w_s": SG_BLOCK ** -0.5 * nrm(ks[8], (N_LAYERS_A, SG_GROUPS, SG_BLOCK, SG_BLOCK), f32),
        "a_b_s": 1.0 + 0.02 * nrm(ks[9], (N_LAYERS_A, SG_GROUPS, SG_BLOCK), f32),
        "a_w_out": DI ** -0.5 * nrm(ks[10], (N_LAYERS_A, DI, D), f32),
        "b_w_in": D ** -0.5 * nrm(ks[11], (N_LAYERS_B, D, 4 * DI), f32),
        "b_lower_bounds": 0.1 * nrm(ks[12], (DEPTH, DI), f32),
        "b_gn_gain": 1.0 + 0.02 * nrm(ks[13], (N_LAYERS_B, HG_HEAD_DIM), f32),
        "b_w_out": DI ** -0.5 * nrm(ks[14], (N_LAYERS_B, DI, D), f32),
        "final_gain": 1.0 + 0.02 * nrm(ks[15], (D,), f32),
    }


def reference(x, c, norm_gain, w_ada, b_ada, a_w_in, a_ln_gain, a_ln_bias, a_w_s, a_b_s,
              a_w_out, b_w_in, b_lower_bounds, b_gn_gain, b_w_out, final_gain):
    p = jax.nn.softmax(b_lower_bounds.astype(jnp.float32), axis=0)
    cum = jnp.cumsum(p, axis=0)
    lower_bounds = cum - cum[0:1]
    c_act = jax.nn.silu(c)
    for layer in range(DEPTH):
        mod = c_act @ w_ada[layer] + b_ada[layer]
        shift = mod[:, None, :D_MODEL]
        scale = mod[:, None, D_MODEL: 2 * D_MODEL]
        gate = mod[:, None, 2 * D_MODEL:]
        h = rms_norm(x, norm_gain[layer]) * (1.0 + scale) + shift
        j = layer // N_MIXERS
        if layer % N_MIXERS == 0:
            y = spatial_gating_mixer(h, a_w_in[j], a_ln_gain[j], a_ln_bias[j],
                                     a_w_s[j], a_b_s[j], a_w_out[j])
        else:
            y = hgrn2_mixer(h, b_w_in[j], lower_bounds[layer], b_gn_gain[j], b_w_out[j])
        x = x + gate * y
    return rms_norm(x, final_gain)
```

```python
import functools
import math

import jax
import jax.numpy as jnp
from jax import lax
from jax.experimental import pallas as pl
from jax.experimental.pallas import tpu as pltpu

D_MODEL = 1024
D_INNER = 2048
CHUNK = 64
SG_BLOCK = 128
SG_GROUPS = 8
SG_GROUP_DIM = D_INNER // SG_GROUPS
HG_HEADS = 16
HG_HEAD_DIM = D_INNER // HG_HEADS
HEAD_PAIRS = HG_HEADS // 2
PAIR_DIM = 2 * HG_HEAD_DIM
EPS = 1e-6

TOKENS_PER_STEP = 256
VMEM_LIMIT_BYTES = 56 * 1024 * 1024

_BF16 = jnp.bfloat16
_F32 = jnp.float32


def _dot(a, b):
    return jnp.dot(a, b, preferred_element_type=_F32)


def _dot_nt(a, b):
    return lax.dot_general(a, b, (((1,), (1,)), ((), ())), preferred_element_type=_F32)


def _dot_tn(a, b):
    return lax.dot_general(a, b, (((0,), (0,)), ((), ())), preferred_element_type=_F32)


def _sigmoid(z):
    return 1.0 / (1.0 + jnp.exp(-z))


def _silu(z):
    return z * _sigmoid(z)


def _gelu_tanh(z):
    c = math.sqrt(2.0 / math.pi)
    return 0.5 * z * (1.0 + jnp.tanh(c * (z + 0.044715 * (z * z * z))))


def _modulated_norm(x, gain, mod_ref):
    r = lax.rsqrt(jnp.mean(x * x, axis=-1, keepdims=True) + EPS)
    shift = mod_ref[0, 0:1, :]
    scale = mod_ref[0, 1:2, :]
    return (x * r * gain) * (1.0 + scale) + shift


def _adaln_kernel(c_ref, w_ref, b_ref, o_ref):
    c_act = _silu(c_ref[...]).astype(_BF16)
    o_ref[0] = _dot(c_act, w_ref[0].astype(_BF16)) + b_ref[0]


def _adaln_mod(c, w_ada, b_ada):
    depth, d, n3 = w_ada.shape
    bsz = c.shape[0]
    nt = n3 // d
    return pl.pallas_call(
        _adaln_kernel,
        grid=(depth, nt),
        in_specs=[
            pl.BlockSpec((bsz, d), lambda l, j: (0, 0)),
            pl.BlockSpec((1, d, d), lambda l, j: (l, 0, j)),
            pl.BlockSpec((1, 1, d), lambda l, j: (l, 0, j)),
        ],
        out_specs=pl.BlockSpec((1, bsz, d), lambda l, j: (l, 0, j)),
        out_shape=jax.ShapeDtypeStruct((depth, bsz, n3), _F32),
        compiler_params=pltpu.CompilerParams(dimension_semantics=("arbitrary", "arbitrary")),
        name="adaln_mod",
    )(c, w_ada, b_ada.reshape(depth, 1, n3))


def _sgu_kernel(x_ref, mod_ref, gain_ref, wv_ref, wug_ref, lng_ref, lnb_ref, ws_ref, bs_ref,
                wout_ref, o_ref, h_ref, vn_ref, acc_ref):
    tm = x_ref.shape[1]
    x = x_ref[0]
    h_ref[...] = _modulated_norm(x, gain_ref[...], mod_ref).astype(_BF16)

    v = _gelu_tanh(_dot(h_ref[...], wv_ref[...]))
    mu = jnp.mean(v, axis=-1, keepdims=True)
    vc = v - mu
    var = jnp.mean(vc * vc, axis=-1, keepdims=True)
    vn = vc * lax.rsqrt(var + EPS) * lng_ref[...] + lnb_ref[...]
    for g in range(SG_GROUPS):
        vn_ref[g] = vn[:, g * SG_GROUP_DIM:(g + 1) * SG_GROUP_DIM].astype(_BF16)

    t_chunk = lax.broadcasted_iota(jnp.int32, (SG_BLOCK, SG_BLOCK), 0) // CHUNK
    s_chunk = lax.broadcasted_iota(jnp.int32, (SG_BLOCK, SG_BLOCK), 1) // CHUNK
    allowed = s_chunk <= t_chunk

    acc_ref[...] = jnp.zeros_like(acc_ref)

    def group_body(g, carry):
        ug = _dot(h_ref[...], wug_ref[g])
        u = _gelu_tanh(ug[:, :SG_GROUP_DIM])
        gate = _silu(ug[:, SG_GROUP_DIM:])
        ws = jnp.where(allowed, ws_ref[g], 0.0).astype(_BF16)
        bias = bs_ref[g]
        s_blocks = []
        for blk in range(tm // SG_BLOCK):
            vb = vn_ref[g, blk * SG_BLOCK:(blk + 1) * SG_BLOCK, :]
            s_blocks.append(_dot(ws, vb) + bias)
        s = jnp.concatenate(s_blocks, axis=0)
        y = (u * s * gate).astype(_BF16)
        acc_ref[...] += _dot(y, wout_ref[g])
        return carry

    lax.fori_loop(0, SG_GROUPS, group_body, 0)
    o_ref[0] = x + mod_ref[0, 2:3, :] * acc_ref[...]


def _sgu_layer(x, mod, gain, wv, wug, ln_gain, ln_bias, w_s, b_s, w_out):
    bsz, seq, d = x.shape
    tm = TOKENS_PER_STEP
    const = lambda nd: (lambda b, t: (0,) * nd)
    resident = functools.partial(pl.BlockSpec, pipeline_mode=pl.Buffered(1))
    return pl.pallas_call(
        _sgu_kernel,
        grid=(bsz, seq // tm),
        in_specs=[
            pl.BlockSpec((1, tm, d), lambda b, t: (b, t, 0)),
            pl.BlockSpec((1, 3, d), lambda b, t: (b, 0, 0)),
            resident((1, d), const(2)),
            resident(wv.shape, const(2)),
            resident(wug.shape, const(3)),
            resident((1, D_INNER), const(2)),
            resident((1, D_INNER), const(2)),
            resident(w_s.shape, const(3)),
            resident(b_s.shape, const(3)),
            resident(w_out.shape, const(3)),
        ],
        out_specs=pl.BlockSpec((1, tm, d), lambda b, t: (b, t, 0)),
        out_shape=jax.ShapeDtypeStruct(x.shape, x.dtype),
        scratch_shapes=[
            pltpu.VMEM((tm, d), _BF16),
            pltpu.VMEM((SG_GROUPS, tm, SG_GROUP_DIM), _BF16),
            pltpu.VMEM((tm, d), _F32),
        ],
        compiler_params=pltpu.CompilerParams(
            dimension_semantics=("arbitrary", "arbitrary"), vmem_limit_bytes=VMEM_LIMIT_BYTES),
        name="sgu_layer",
    )(x, mod, gain, wv, wug, ln_gain, ln_bias, w_s, b_s, w_out)


def _hgrn2_kernel(x_ref, mod_ref, gain_ref, w_ref, lbraw_ref, gn_ref, wout_ref, fgain_ref,
                  o_ref, h_ref, acc_ref, state_ref, *, layer):
    tm = x_ref.shape[1]
    n_chunks = tm // CHUNK

    @pl.when(pl.program_id(1) == 0)
    def _():
        state_ref[...] = jnp.zeros_like(state_ref)

    x = x_ref[0]
    h_ref[...] = _modulated_norm(x, gain_ref[...], mod_ref).astype(_BF16)
    acc_ref[...] = jnp.zeros_like(acc_ref)

    row = lax.broadcasted_iota(jnp.int32, (tm, tm), 0)
    col = lax.broadcasted_iota(jnp.int32, (tm, tm), 1)
    prefix = jnp.where((row // CHUNK == col // CHUNK) & (col <= row), 1.0, 0.0).astype(_BF16)
    crow = lax.broadcasted_iota(jnp.int32, (CHUNK, CHUNK), 0)
    ccol = lax.broadcasted_iota(jnp.int32, (CHUNK, CHUNK), 1)
    causal = ccol <= crow
    gn = gn_ref[...]

    def pair_body(hp, carry):
        proj = _dot(h_ref[...], w_ref[hp])
        q = _silu(proj[:, 0 * PAIR_DIM:1 * PAIR_DIM])
        fz = proj[:, 1 * PAIR_DIM:2 * PAIR_DIM]
        vv = proj[:, 2 * PAIR_DIM:3 * PAIR_DIM].astype(_BF16)
        og = _silu(proj[:, 3 * PAIR_DIM:4 * PAIR_DIM])

        lbraw = lbraw_ref[hp]
        e = jnp.exp(lbraw - jnp.max(lbraw, axis=0, keepdims=True))
        p = e / jnp.sum(e, axis=0, keepdims=True)
        lb = jnp.sum(p[1:layer + 1], axis=0, keepdims=True)

        f = lb + (1.0 - lb) * _sigmoid(fz)
        k = 1.0 - f
        log_f = jnp.log(f)
        lf_hi = log_f.astype(_BF16)
        r1 = log_f - lf_hi.astype(_F32)
        lf_mid = r1.astype(_BF16)
        lf_lo = (r1 - lf_mid.astype(_F32)).astype(_BF16)
        a = _dot(prefix, lf_hi) + _dot(prefix, lf_mid) + _dot(prefix, lf_lo)

        o_heads = []
        for hd in range(2):
            lanes = slice(hd * HG_HEAD_DIM, (hd + 1) * HG_HEAD_DIM)
            sidx = hp * 2 + hd
            st = state_ref[sidx]
            o_chunks = []
            for c in range(n_chunks):
                rows = slice(c * CHUNK, (c + 1) * CHUNK)
                a_c = a[rows, lanes]
                mid = c * CHUNK + CHUNK // 2 - 1
                last = c * CHUNK + CHUNK - 1
                a_ref = a[mid:mid + 1, lanes]
                a_last = a[last:last + 1, lanes]
                q_in = q[rows, lanes] * jnp.exp(a_c - a_ref)
                k_in = k[rows, lanes] * jnp.exp(a_ref - a_c)
                q_out = (q_in * jnp.exp(a_ref)).astype(_BF16)
                k_out = (k_in * jnp.exp(a_last - a_ref)).astype(_BF16)
                decay = jnp.exp(a_last)
                v_c = vv[rows, lanes]
                scores = _dot_nt(q_in.astype(_BF16), k_in.astype(_BF16))
                scores = jnp.where(causal, scores, 0.0).astype(_BF16)
                o_c = _dot(scores, v_c) + _dot_nt(q_out, st.astype(_BF16))
                st = decay * st + _dot_tn(v_c, k_out)
                o_chunks.append(o_c)
            state_ref[sidx] = st
            o = jnp.concatenate(o_chunks, axis=0)
            o = o * lax.rsqrt(jnp.mean(o * o, axis=-1, keepdims=True) + EPS) * gn
            o_heads.append(o)
        y = (jnp.concatenate(o_heads, axis=1) * og).astype(_BF16)
        acc_ref[...] += _dot(y, wout_ref[hp])
        return carry

    lax.fori_loop(0, HEAD_PAIRS, pair_body, 0)
    xo = x + mod_ref[0, 2:3, :] * acc_ref[...]
    r = lax.rsqrt(jnp.mean(xo * xo, axis=-1, keepdims=True) + EPS)
    o_ref[0] = xo * r * fgain_ref[...]


def _hgrn2_layer(x, mod, gain, w, lb_raw, gn_gain, w_out, final_gain, layer):
    bsz, seq, d = x.shape
    tm = TOKENS_PER_STEP
    const = lambda nd: (lambda b, t: (0,) * nd)
    resident = functools.partial(pl.BlockSpec, pipeline_mode=pl.Buffered(1))
    return pl.pallas_call(
        functools.partial(_hgrn2_kernel, layer=layer),
        grid=(bsz, seq // tm),
        in_specs=[
            pl.BlockSpec((1, tm, d), lambda b, t: (b, t, 0)),
            pl.BlockSpec((1, 3, d), lambda b, t: (b, 0, 0)),
            resident((1, d), const(2)),
            resident(w.shape, const(3)),
            resident(lb_raw.shape, const(3)),
            resident((1, HG_HEAD_DIM), const(2)),
            resident(w_out.shape, const(3)),
            resident((1, d), const(2)),
        ],
        out_specs=pl.BlockSpec((1, tm, d), lambda b, t: (b, t, 0)),
        out_shape=jax.ShapeDtypeStruct(x.shape, x.dtype),
        scratch_shapes=[
            pltpu.VMEM((tm, d), _BF16),
            pltpu.VMEM((tm, d), _F32),
            pltpu.VMEM((HG_HEADS, HG_HEAD_DIM, HG_HEAD_DIM), _F32),
        ],
        compiler_params=pltpu.CompilerParams(
            dimension_semantics=("arbitrary", "arbitrary"), vmem_limit_bytes=VMEM_LIMIT_BYTES),
        name="hgrn2_layer",
    )(x, mod, gain, w, lb_raw, gn_gain, w_out, final_gain)


def kernel(x, c, norm_gain, w_ada, b_ada, a_w_in, a_ln_gain, a_ln_bias, a_w_s, a_b_s, a_w_out,
           b_w_in, b_lower_bounds, b_gn_gain, b_w_out, final_gain):
    bsz, seq, d = x.shape
    depth = norm_gain.shape[0]
    assert depth == 2 and a_w_in.shape[0] == 1 and b_w_in.shape[0] == 1
    assert seq % TOKENS_PER_STEP == 0 and TOKENS_PER_STEP % SG_BLOCK == 0

    mod = _adaln_mod(c, w_ada, b_ada)
    mod = mod.reshape(depth, bsz, 3, d)

    wa = a_w_in[0].astype(_BF16)
    wu = wa[:, :D_INNER].reshape(d, SG_GROUPS, SG_GROUP_DIM)
    wv = wa[:, D_INNER:2 * D_INNER]
    wg = wa[:, 2 * D_INNER:].reshape(d, SG_GROUPS, SG_GROUP_DIM)
    wug = jnp.concatenate([wu, wg], axis=-1).transpose(1, 0, 2)
    wout_a = a_w_out[0].astype(_BF16).reshape(SG_GROUPS, SG_GROUP_DIM, d)
    x1 = _sgu_layer(
        x, mod[0], norm_gain[0:1], wv, wug, a_ln_gain, a_ln_bias, a_w_s[0],
        a_b_s[0].reshape(SG_GROUPS, SG_BLOCK, 1), wout_a)

    wb = b_w_in[0].astype(_BF16).reshape(d, 4, HEAD_PAIRS, PAIR_DIM)
    wb = wb.transpose(2, 0, 1, 3).reshape(HEAD_PAIRS, d, 4 * PAIR_DIM)
    lb_raw = b_lower_bounds.reshape(depth, HEAD_PAIRS, PAIR_DIM).transpose(1, 0, 2)
    wout_b = b_w_out[0].astype(_BF16).reshape(HEAD_PAIRS, PAIR_DIM, d)
    return _hgrn2_layer(
        x1, mod[1], norm_gain[1:2], wb, lb_raw, b_gn_gain, wout_b, final_gain.reshape(1, d),
        layer=1)
```

```python
import functools
import math

import jax
import jax.numpy as jnp
from jax import lax
from jax.experimental import pallas as pl
from jax.experimental.pallas import tpu as pltpu

D_MODEL = 1024
D_INNER = 2048
CHUNK = 64
SG_BLOCK = 128
SG_GROUPS = 8
SG_GROUP_DIM = D_INNER // SG_GROUPS
HG_HEADS = 16
HG_HEAD_DIM = D_INNER // HG_HEADS
HEAD_PAIRS = HG_HEADS // 2
PAIR_DIM = 2 * HG_HEAD_DIM
EPS = 1e-6

TOKENS_PER_STEP = 256
VMEM_LIMIT_BYTES = 56 * 1024 * 1024

_BF16 = jnp.bfloat16
_F32 = jnp.float32


def _dot(a, b):
    return jnp.dot(a, b, preferred_element_type=_F32)


def _dot_nt(a, b):
    return lax.dot_general(a, b, (((1,), (1,)), ((), ())), preferred_element_type=_F32)


def _dot_tn(a, b):
    return lax.dot_general(a, b, (((0,), (0,)), ((), ())), preferred_element_type=_F32)


def _sigmoid(z):
    return 1.0 / (1.0 + jnp.exp(-z))


def _silu(z):
    return z * _sigmoid(z)


def _gelu_tanh(z):
    c = math.sqrt(2.0 / math.pi)
    return 0.5 * z * (1.0 + jnp.tanh(c * (z + 0.044715 * (z * z * z))))


def _modulated_norm(x, gain, mod_ref):
    r = lax.rsqrt(jnp.mean(x * x, axis=-1, keepdims=True) + EPS)
    shift = mod_ref[0, 0:1, :]
    scale = mod_ref[0, 1:2, :]
    return (x * r * gain) * (1.0 + scale) + shift


def _adaln_kernel(c_ref, w_ref, b_ref, o_ref):
    c_act = _silu(c_ref[...]).astype(_BF16)
    o_ref[0] = _dot(c_act, w_ref[0].astype(_BF16)) + b_ref[0]


def _adaln_mod(c, w_ada, b_ada):
    depth, d, n3 = w_ada.shape
    bsz = c.shape[0]
    nt = n3 // d
    return pl.pallas_call(
        _adaln_kernel,
        grid=(depth, nt),
        in_specs=[
            pl.BlockSpec((bsz, d), lambda l, j: (0, 0)),
            pl.BlockSpec((1, d, d), lambda l, j: (l, 0, j)),
            pl.BlockSpec((1, 1, d), lambda l, j: (l, 0, j)),
        ],
        out_specs=pl.BlockSpec((1, bsz, d), lambda l, j: (l, 0, j)),
        out_shape=jax.ShapeDtypeStruct((depth, bsz, n3), _F32),
        compiler_params=pltpu.CompilerParams(dimension_semantics=("arbitrary", "arbitrary")),
        name="adaln_mod",
    )(c, w_ada, b_ada.reshape(depth, 1, n3))


def _sgu_kernel(x_ref, mod_ref, gain_ref, wv_ref, wug_ref, lng_ref, lnb_ref, ws_ref, bs_ref,
                wout_ref, o_ref, h_ref, vn_ref, acc_ref):
    tm = x_ref.shape[1]
    x = x_ref[0]
    h_ref[...] = _modulated_norm(x, gain_ref[...], mod_ref).astype(_BF16)

    v = _gelu_tanh(_dot(h_ref[...], wv_ref[...]))
    mu = jnp.mean(v, axis=-1, keepdims=True)
    vc = v - mu
    var = jnp.mean(vc * vc, axis=-1, keepdims=True)
    vn = vc * lax.rsqrt(var + EPS) * lng_ref[...] + lnb_ref[...]
    for g in range(SG_GROUPS):
        vn_ref[g] = vn[:, g * SG_GROUP_DIM:(g + 1) * SG_GROUP_DIM].astype(_BF16)

    t_chunk = lax.broadcasted_iota(jnp.int32, (SG_BLOCK, SG_BLOCK), 0) // CHUNK
    s_chunk = lax.broadcasted_iota(jnp.int32, (SG_BLOCK, SG_BLOCK), 1) // CHUNK
    allowed = s_chunk <= t_chunk

    acc_ref[...] = jnp.zeros_like(acc_ref)

    def group_body(g, carry):
        ug = _dot(h_ref[...], wug_ref[g])
        u = _gelu_tanh(ug[:, :SG_GROUP_DIM])
        gate = _silu(ug[:, SG_GROUP_DIM:])
        ws = jnp.where(allowed, ws_ref[g], 0.0).astype(_BF16)
        bias = bs_ref[g]
        s_blocks = []
        for blk in range(tm // SG_BLOCK):
            vb = vn_ref[g, blk * SG_BLOCK:(blk + 1) * SG_BLOCK, :]
            s_blocks.append(_dot(ws, vb) + bias)
        s = jnp.concatenate(s_blocks, axis=0)
        y = (u * s * gate).astype(_BF16)
        acc_ref[...] += _dot(y, wout_ref[g])
        return carry

    lax.fori_loop(0, SG_GROUPS, group_body, 0)
    o_ref[0] = x + mod_ref[0, 2:3, :] * acc_ref[...]


def _sgu_layer(x, mod, gain, wv, wug, ln_gain, ln_bias, w_s, b_s, w_out):
    bsz, seq, d = x.shape
    tm = TOKENS_PER_STEP
    const = lambda nd: (lambda b, t: (0,) * nd)
    resident = functools.partial(pl.BlockSpec, pipeline_mode=pl.Buffered(1))
    return pl.pallas_call(
        _sgu_kernel,
        grid=(bsz, seq // tm),
        in_specs=[
            pl.BlockSpec((1, tm, d), lambda b, t: (b, t, 0)),
            pl.BlockSpec((1, 3, d), lambda b, t: (b, 0, 0)),
            resident((1, d), const(2)),
            resident(wv.shape, const(2)),
            resident(wug.shape, const(3)),
            resident((1, D_INNER), const(2)),
            resident((1, D_INNER), const(2)),
            resident(w_s.shape, const(3)),
            resident(b_s.shape, const(3)),
            resident(w_out.shape, const(3)),
        ],
        out_specs=pl.BlockSpec((1, tm, d), lambda b, t: (b, t, 0)),
        out_shape=jax.ShapeDtypeStruct(x.shape, x.dtype),
        scratch_shapes=[
            pltpu.VMEM((tm, d), _BF16),
            pltpu.VMEM((SG_GROUPS, tm, SG_GROUP_DIM), _BF16),
            pltpu.VMEM((tm, d), _F32),
        ],
        compiler_params=pltpu.CompilerParams(
            dimension_semantics=("arbitrary", "arbitrary"), vmem_limit_bytes=VMEM_LIMIT_BYTES),
        name="sgu_layer",
    )(x, mod, gain, wv, wug, ln_gain, ln_bias, w_s, b_s, w_out)


def _hgrn2_kernel(x_ref, mod_ref, gain_ref, w_ref, lbraw_ref, gn_ref, wout_ref, fgain_ref,
                  o_ref, h_ref, acc_ref, state_ref, proj_ref, *, layer):
    tm = x_ref.shape[1]
    n_chunks = tm // CHUNK
    Q, F, I, G = range(4)

    @pl.when(pl.program_id(1) == 0)
    def _():
        state_ref[...] = jnp.zeros_like(state_ref)

    x = x_ref[0]
    h_ref[...] = _modulated_norm(x, gain_ref[...], mod_ref).astype(_BF16)

    row = lax.broadcasted_iota(jnp.int32, (tm, tm), 0)
    col = lax.broadcasted_iota(jnp.int32, (tm, tm), 1)
    intra = (row // CHUNK == col // CHUNK) & (col <= row)
    prefix = jnp.where(intra, 1.0, 0.0).astype(_BF16)
    gn = gn_ref[...]
    heads = [slice(hd * HG_HEAD_DIM, (hd + 1) * HG_HEAD_DIM) for hd in range(2)]
    chunks = [slice(c * CHUNK, (c + 1) * CHUNK) for c in range(n_chunks)]

    def project(hp, part):
        cols = slice(part * PAIR_DIM, (part + 1) * PAIR_DIM)
        proj_ref[hp % 2, part] = _dot(h_ref[...], w_ref[hp, :, cols])

    def forget_gate(hp):
        lbraw = lbraw_ref[hp]
        e = jnp.exp(lbraw - jnp.max(lbraw, axis=0, keepdims=True))
        p = e / jnp.sum(e, axis=0, keepdims=True)
        lb = jnp.sum(p[1:layer + 1], axis=0, keepdims=True)
        f = lb + (1.0 - lb) * _sigmoid(proj_ref[hp % 2, F])
        log_f = jnp.log(f)
        lf_hi = log_f.astype(_BF16)
        r1 = log_f - lf_hi.astype(_F32)
        lf_mid = r1.astype(_BF16)
        lf_lo = (r1 - lf_mid.astype(_F32)).astype(_BF16)
        return 1.0 - f, (lf_hi, lf_mid, lf_lo)

    def prefix_sum(terms):
        return _dot(prefix, terms[0]) + _dot(prefix, terms[1]) + _dot(prefix, terms[2])

    def decay_scaling(hp, k, a):
        def chunk_rows(offset):
            return jnp.concatenate(
                [jnp.broadcast_to(a[c * CHUNK + offset:c * CHUNK + offset + 1, :], (CHUNK, PAIR_DIM))
                 for c in range(n_chunks)], axis=0)
        a_mid = chunk_rows(CHUNK // 2 - 1)
        a_end = chunk_rows(CHUNK - 1)
        q = _silu(proj_ref[hp % 2, Q])
        q_in = q * jnp.exp(a - a_mid)
        k_in = k * jnp.exp(a_mid - a)
        q_out = (q_in * jnp.exp(a_mid)).astype(_BF16)
        k_out = (k_in * jnp.exp(a_end - a_mid)).astype(_BF16)
        decay = [jnp.exp(a[c * CHUNK + CHUNK - 1:c * CHUNK + CHUNK, :]) for c in range(n_chunks)]
        return q_in.astype(_BF16), k_in.astype(_BF16), q_out, k_out, decay

    def scores_and_updates(hp, q_in, k_in, k_out):
        vv = proj_ref[hp % 2, I].astype(_BF16)
        scores = [_dot_nt(q_in[:, ln], k_in[:, ln]) for ln in heads]
        kv = [[_dot_tn(vv[rw, ln], k_out[rw, ln]) for rw in chunks] for ln in heads]
        return vv, scores, kv

    def mask_and_states(hp, scores, kv, decay):
        scores = [jnp.where(intra, s, 0.0).astype(_BF16) for s in scores]
        states = []
        for hd, ln in enumerate(heads):
            st = state_ref[hp * 2 + hd]
            per_chunk = []
            for c in range(n_chunks):
                per_chunk.append(st.astype(_BF16))
                st = decay[c][:, ln] * st + kv[hd][c]
            state_ref[hp * 2 + hd] = st
            states.append(per_chunk)
        return scores, states

    def outputs(vv, scores, q_out, states):
        o_heads = []
        for hd, ln in enumerate(heads):
            o_inter = [_dot_nt(q_out[rw, ln], states[hd][c]) for c, rw in enumerate(chunks)]
            o_heads.append(_dot(scores[hd], vv[:, ln]) + jnp.concatenate(o_inter, axis=0))
        return o_heads

    def gated_norm(hp, o_heads):
        og = _silu(proj_ref[hp % 2, G])
        normed = [o * lax.rsqrt(jnp.mean(o * o, axis=-1, keepdims=True) + EPS) * gn for o in o_heads]
        return (jnp.concatenate(normed, axis=1) * og).astype(_BF16)

    def out_project(hp, y):
        contrib = _dot(y, wout_ref[hp])
        if hp == 0:
            acc_ref[...] = contrib
        else:
            acc_ref[...] += contrib

    for part in (F, Q, I, G):
        project(0, part)
    for hp in range(HEAD_PAIRS):
        nxt = hp + 1 if hp + 1 < HEAD_PAIRS else None
        k, terms = forget_gate(hp)
        a = prefix_sum(terms)
        if nxt is not None:
            project(nxt, F)
        q_in, k_in, q_out, k_out, decay = decay_scaling(hp, k, a)
        vv, scores, kv = scores_and_updates(hp, q_in, k_in, k_out)
        if nxt is not None:
            project(nxt, Q)
        scores, states = mask_and_states(hp, scores, kv, decay)
        o_heads = outputs(vv, scores, q_out, states)
        if nxt is not None:
            project(nxt, I)
        y = gated_norm(hp, o_heads)
        out_project(hp, y)
        if nxt is not None:
            project(nxt, G)

    xo = x + mod_ref[0, 2:3, :] * acc_ref[...]
    r = lax.rsqrt(jnp.mean(xo * xo, axis=-1, keepdims=True) + EPS)
    o_ref[0] = xo * r * fgain_ref[...]


def _hgrn2_layer(x, mod, gain, w, lb_raw, gn_gain, w_out, final_gain, layer):
    bsz, seq, d = x.shape
    tm = TOKENS_PER_STEP
    const = lambda nd: (lambda b, t: (0,) * nd)
    resident = functools.partial(pl.BlockSpec, pipeline_mode=pl.Buffered(1))
    return pl.pallas_call(
        functools.partial(_hgrn2_kernel, layer=layer),
        grid=(bsz, seq // tm),
        in_specs=[
            pl.BlockSpec((1, tm, d), lambda b, t: (b, t, 0)),
            pl.BlockSpec((1, 3, d), lambda b, t: (b, 0, 0)),
            resident((1, d), const(2)),
            resident(w.shape, const(3)),
            resident(lb_raw.shape, const(3)),
            resident((1, HG_HEAD_DIM), const(2)),
            resident(w_out.shape, const(3)),
            resident((1, d), const(2)),
        ],
        out_specs=pl.BlockSpec((1, tm, d), lambda b, t: (b, t, 0)),
        out_shape=jax.ShapeDtypeStruct(x.shape, x.dtype),
        scratch_shapes=[
            pltpu.VMEM((tm, d), _BF16),
            pltpu.VMEM((tm, d), _F32),
            pltpu.VMEM((HG_HEADS, HG_HEAD_DIM, HG_HEAD_DIM), _F32),
            pltpu.VMEM((2, 4, tm, PAIR_DIM), _F32),
        ],
        compiler_params=pltpu.CompilerParams(
            dimension_semantics=("arbitrary", "arbitrary"), vmem_limit_bytes=VMEM_LIMIT_BYTES),
        name="hgrn2_layer",
    )(x, mod, gain, w, lb_raw, gn_gain, w_out, final_gain)


def kernel(x, c, norm_gain, w_ada, b_ada, a_w_in, a_ln_gain, a_ln_bias, a_w_s, a_b_s, a_w_out,
           b_w_in, b_lower_bounds, b_gn_gain, b_w_out, final_gain):
    bsz, seq, d = x.shape
    depth = norm_gain.shape[0]
    assert depth == 2 and a_w_in.shape[0] == 1 and b_w_in.shape[0] == 1
    assert seq % TOKENS_PER_STEP == 0 and TOKENS_PER_STEP % SG_BLOCK == 0

    mod = _adaln_mod(c, w_ada, b_ada)
    mod = mod.reshape(depth, bsz, 3, d)

    wa = a_w_in[0].astype(_BF16)
    wu = wa[:, :D_INNER].reshape(d, SG_GROUPS, SG_GROUP_DIM)
    wv = wa[:, D_INNER:2 * D_INNER]
    wg = wa[:, 2 * D_INNER:].reshape(d, SG_GROUPS, SG_GROUP_DIM)
    wug = jnp.concatenate([wu, wg], axis=-1).transpose(1, 0, 2)
    wout_a = a_w_out[0].astype(_BF16).reshape(SG_GROUPS, SG_GROUP_DIM, d)
    x1 = _sgu_layer(
        x, mod[0], norm_gain[0:1], wv, wug, a_ln_gain, a_ln_bias, a_w_s[0],
        a_b_s[0].reshape(SG_GROUPS, SG_BLOCK, 1), wout_a)

    wb = b_w_in[0].astype(_BF16).reshape(d, 4, HEAD_PAIRS, PAIR_DIM)
    wb = wb.transpose(2, 0, 1, 3).reshape(HEAD_PAIRS, d, 4 * PAIR_DIM)
    lb_raw = b_lower_bounds.reshape(depth, HEAD_PAIRS, PAIR_DIM).transpose(1, 0, 2)
    wout_b = b_w_out[0].astype(_BF16).reshape(HEAD_PAIRS, PAIR_DIM, d)
    return _hgrn2_layer(
        x1, mod[1], norm_gain[1:2], wb, lb_raw, b_gn_gain, wout_b, final_gain.reshape(1, d),
        layer=1)
```

```python
import functools
import math

import jax
import jax.numpy as jnp
from jax import lax
from jax.experimental import pallas as pl
from jax.experimental.pallas import tpu as pltpu

D_MODEL = 1024
D_INNER = 2048
CHUNK = 64
SG_BLOCK = 128
SG_GROUPS = 8
SG_GROUP_DIM = D_INNER // SG_GROUPS
HG_HEADS = 16
HG_HEAD_DIM = D_INNER // HG_HEADS
HEAD_PAIRS = HG_HEADS // 2
PAIR_DIM = 2 * HG_HEAD_DIM
EPS = 1e-6

TOKENS_PER_STEP = 256
VMEM_LIMIT_BYTES = 56 * 1024 * 1024

_BF16 = jnp.bfloat16
_F32 = jnp.float32


def _dot(a, b):
    return jnp.dot(a, b, preferred_element_type=_F32)


def _dot_nt(a, b):
    return lax.dot_general(a, b, (((1,), (1,)), ((), ())), preferred_element_type=_F32)


def _dot_tn(a, b):
    return lax.dot_general(a, b, (((0,), (0,)), ((), ())), preferred_element_type=_F32)


def _sigmoid(z):
    return 1.0 / (1.0 + jnp.exp(-z))


def _silu(z):
    return z * _sigmoid(z)


def _gelu_tanh(z):
    c = math.sqrt(2.0 / math.pi)
    return 0.5 * z * (1.0 + jnp.tanh(c * (z + 0.044715 * (z * z * z))))


def _modulated_norm(x, gain, mod_ref):
    r = lax.rsqrt(jnp.mean(x * x, axis=-1, keepdims=True) + EPS)
    shift = mod_ref[0, 0:1, :]
    scale = mod_ref[0, 1:2, :]
    return (x * r * gain) * (1.0 + scale) + shift


def _adaln_kernel(c_ref, w_ref, b_ref, o_ref):
    c_act = _silu(c_ref[...]).astype(_BF16)
    o_ref[0] = _dot(c_act, w_ref[0].astype(_BF16)) + b_ref[0]


def _adaln_mod(c, w_ada, b_ada):
    depth, d, n3 = w_ada.shape
    bsz = c.shape[0]
    nt = n3 // d
    return pl.pallas_call(
        _adaln_kernel,
        grid=(depth, nt),
        in_specs=[
            pl.BlockSpec((bsz, d), lambda l, j: (0, 0)),
            pl.BlockSpec((1, d, d), lambda l, j: (l, 0, j)),
            pl.BlockSpec((1, 1, d), lambda l, j: (l, 0, j)),
        ],
        out_specs=pl.BlockSpec((1, bsz, d), lambda l, j: (l, 0, j)),
        out_shape=jax.ShapeDtypeStruct((depth, bsz, n3), _F32),
        compiler_params=pltpu.CompilerParams(dimension_semantics=("arbitrary", "arbitrary")),
        name="adaln_mod",
    )(c, w_ada, b_ada.reshape(depth, 1, n3))


def _sgu_kernel(x_ref, mod_ref, gain_ref, wv_ref, wug_ref, lng_ref, lnb_ref, ws_ref, bs_ref,
                wout_ref, o_ref, h_ref, v_ref, ug_ref, acc_ref):
    tm = x_ref.shape[1]
    x = x_ref[0]
    h_ref[...] = _modulated_norm(x, gain_ref[...], mod_ref).astype(_BF16)

    t_chunk = lax.broadcasted_iota(jnp.int32, (SG_BLOCK, SG_BLOCK), 0) // CHUNK
    s_chunk = lax.broadcasted_iota(jnp.int32, (SG_BLOCK, SG_BLOCK), 1) // CHUNK
    allowed = s_chunk <= t_chunk

    def group_cols(g):
        return slice(g * SG_GROUP_DIM, (g + 1) * SG_GROUP_DIM)

    def project_ug(g):
        ug_ref[g % 2] = _dot(h_ref[...], wug_ref[g])

    def lane_halves_sum(z):
        return z[:, :SG_GROUP_DIM // 2] + z[:, SG_GROUP_DIM // 2:]

    row_sum = None
    for g in range(SG_GROUPS):
        vg = _gelu_tanh(_dot(h_ref[...], wv_ref[:, group_cols(g)]))
        v_ref[g] = vg
        row_sum = lane_halves_sum(vg) if row_sum is None else row_sum + lane_halves_sum(vg)
    mu = jnp.sum(row_sum, axis=-1, keepdims=True) * (1.0 / D_INNER)
    project_ug(0)
    sq_sum = None
    for g in range(SG_GROUPS):
        vc = v_ref[g] - mu
        sq_sum = lane_halves_sum(vc * vc) if sq_sum is None else sq_sum + lane_halves_sum(vc * vc)
    var = jnp.sum(sq_sum, axis=-1, keepdims=True) * (1.0 / D_INNER)
    rstd = lax.rsqrt(var + EPS)

    for g in range(SG_GROUPS):
        vn = ((v_ref[g] - mu) * rstd * lng_ref[:, group_cols(g)] + lnb_ref[:, group_cols(g)]).astype(_BF16)
        ws = jnp.where(allowed, ws_ref[g], 0.0).astype(_BF16)
        bias = bs_ref[g]
        s = jnp.concatenate(
            [_dot(ws, vn[blk * SG_BLOCK:(blk + 1) * SG_BLOCK, :]) + bias for blk in range(tm // SG_BLOCK)],
            axis=0)
        if g + 1 < SG_GROUPS:
            project_ug(g + 1)
        u = _gelu_tanh(ug_ref[g % 2, :, :SG_GROUP_DIM])
        gate = _silu(ug_ref[g % 2, :, SG_GROUP_DIM:])
        y = (u * s * gate).astype(_BF16)
        contrib = _dot(y, wout_ref[g])
        if g == 0:
            acc_ref[...] = contrib
        else:
            acc_ref[...] += contrib

    o_ref[0] = x + mod_ref[0, 2:3, :] * acc_ref[...]


def _sgu_layer(x, mod, gain, wv, wug, ln_gain, ln_bias, w_s, b_s, w_out):
    bsz, seq, d = x.shape
    tm = TOKENS_PER_STEP
    const = lambda nd: (lambda b, t: (0,) * nd)
    resident = functools.partial(pl.BlockSpec, pipeline_mode=pl.Buffered(1))
    return pl.pallas_call(
        _sgu_kernel,
        grid=(bsz, seq // tm),
        in_specs=[
            pl.BlockSpec((1, tm, d), lambda b, t: (b, t, 0)),
            pl.BlockSpec((1, 3, d), lambda b, t: (b, 0, 0)),
            resident((1, d), const(2)),
            resident(wv.shape, const(2)),
            resident(wug.shape, const(3)),
            resident((1, D_INNER), const(2)),
            resident((1, D_INNER), const(2)),
            resident(w_s.shape, const(3)),
            resident(b_s.shape, const(3)),
            resident(w_out.shape, const(3)),
        ],
        out_specs=pl.BlockSpec((1, tm, d), lambda b, t: (b, t, 0)),
        out_shape=jax.ShapeDtypeStruct(x.shape, x.dtype),
        scratch_shapes=[
            pltpu.VMEM((tm, d), _BF16),
            pltpu.VMEM((SG_GROUPS, tm, SG_GROUP_DIM), _F32),
            pltpu.VMEM((2, tm, 2 * SG_GROUP_DIM), _F32),
            pltpu.VMEM((tm, d), _F32),
        ],
        compiler_params=pltpu.CompilerParams(
            dimension_semantics=("arbitrary", "arbitrary"), vmem_limit_bytes=VMEM_LIMIT_BYTES),
        name="sgu_layer",
    )(x, mod, gain, wv, wug, ln_gain, ln_bias, w_s, b_s, w_out)


def _hgrn2_kernel(x_ref, mod_ref, gain_ref, w_ref, lbraw_ref, gn_ref, wout_ref, fgain_ref,
                  o_ref, h_ref, acc_ref, state_ref, proj_ref, *, layer):
    tm = x_ref.shape[1]
    n_chunks = tm // CHUNK
    Q, F, I, G = range(4)

    @pl.when(pl.program_id(1) == 0)
    def _():
        state_ref[...] = jnp.zeros_like(state_ref)

    x = x_ref[0]
    h_ref[...] = _modulated_norm(x, gain_ref[...], mod_ref).astype(_BF16)

    row = lax.broadcasted_iota(jnp.int32, (tm, tm), 0)
    col = lax.broadcasted_iota(jnp.int32, (tm, tm), 1)
    intra = (row // CHUNK == col // CHUNK) & (col <= row)
    prefix = jnp.where(intra, 1.0, 0.0).astype(_BF16)
    gn = gn_ref[...]
    heads = [slice(hd * HG_HEAD_DIM, (hd + 1) * HG_HEAD_DIM) for hd in range(2)]
    chunks = [slice(c * CHUNK, (c + 1) * CHUNK) for c in range(n_chunks)]

    def project(hp, part):
        cols = slice(part * PAIR_DIM, (part + 1) * PAIR_DIM)
        proj_ref[hp % 2, part] = _dot(h_ref[...], w_ref[hp, :, cols])

    def forget_gate(hp):
        lbraw = lbraw_ref[hp]
        e = jnp.exp(lbraw - jnp.max(lbraw, axis=0, keepdims=True))
        p = e / jnp.sum(e, axis=0, keepdims=True)
        lb = jnp.sum(p[1:layer + 1], axis=0, keepdims=True)
        f = lb + (1.0 - lb) * _sigmoid(proj_ref[hp % 2, F])
        log_f = jnp.log(f)
        lf_hi = log_f.astype(_BF16)
        r1 = log_f - lf_hi.astype(_F32)
        lf_mid = r1.astype(_BF16)
        lf_lo = (r1 - lf_mid.astype(_F32)).astype(_BF16)
        return 1.0 - f, (lf_hi, lf_mid, lf_lo)

    def prefix_sum(terms):
        return _dot(prefix, terms[0]) + _dot(prefix, terms[1]) + _dot(prefix, terms[2])

    def decay_scaling(hp, k, a):
        def chunk_rows(offset):
            return jnp.concatenate(
                [jnp.broadcast_to(a[c * CHUNK + offset:c * CHUNK + offset + 1, :], (CHUNK, PAIR_DIM))
                 for c in range(n_chunks)], axis=0)
        a_mid = chunk_rows(CHUNK // 2 - 1)
        a_end = chunk_rows(CHUNK - 1)
        q = _silu(proj_ref[hp % 2, Q])
        q_in = q * jnp.exp(a - a_mid)
        k_in = k * jnp.exp(a_mid - a)
        q_out = (q_in * jnp.exp(a_mid)).astype(_BF16)
        k_out = (k_in * jnp.exp(a_end - a_mid)).astype(_BF16)
        decay = [jnp.exp(a[c * CHUNK + CHUNK - 1:c * CHUNK + CHUNK, :]) for c in range(n_chunks)]
        return q_in.astype(_BF16), k_in.astype(_BF16), q_out, k_out, decay

    def scores_and_updates(hp, q_in, k_in, k_out):
        vv = proj_ref[hp % 2, I].astype(_BF16)
        scores = [_dot_nt(q_in[:, ln], k_in[:, ln]) for ln in heads]
        kv = [[_dot_tn(vv[rw, ln], k_out[rw, ln]) for rw in chunks] for ln in heads]
        return vv, scores, kv

    def mask_and_states(hp, scores, kv, decay):
        scores = [jnp.where(intra, s, 0.0).astype(_BF16) for s in scores]
        states = []
        for hd, ln in enumerate(heads):
            st = state_ref[hp * 2 + hd]
            per_chunk = []
            for c in range(n_chunks):
                per_chunk.append(st.astype(_BF16))
                st = decay[c][:, ln] * st + kv[hd][c]
            state_ref[hp * 2 + hd] = st
            states.append(per_chunk)
        return scores, states

    def outputs(vv, scores, q_out, states):
        o_heads = []
        for hd, ln in enumerate(heads):
            o_inter = [_dot_nt(q_out[rw, ln], states[hd][c]) for c, rw in enumerate(chunks)]
            o_heads.append(_dot(scores[hd], vv[:, ln]) + jnp.concatenate(o_inter, axis=0))
        return o_heads

    def gated_norm(hp, o_heads):
        og = _silu(proj_ref[hp % 2, G])
        normed = [o * lax.rsqrt(jnp.mean(o * o, axis=-1, keepdims=True) + EPS) * gn for o in o_heads]
        return (jnp.concatenate(normed, axis=1) * og).astype(_BF16)

    def out_project(hp, y):
        contrib = _dot(y, wout_ref[hp])
        if hp == 0:
            acc_ref[...] = contrib
        else:
            acc_ref[...] += contrib

    for part in (F, Q, I, G):
        project(0, part)
    for hp in range(HEAD_PAIRS):
        nxt = hp + 1 if hp + 1 < HEAD_PAIRS else None
        k, terms = forget_gate(hp)
        a = prefix_sum(terms)
        if nxt is not None:
            project(nxt, F)
        q_in, k_in, q_out, k_out, decay = decay_scaling(hp, k, a)
        vv, scores, kv = scores_and_updates(hp, q_in, k_in, k_out)
        if nxt is not None:
            project(nxt, Q)
        scores, states = mask_and_states(hp, scores, kv, decay)
        o_heads = outputs(vv, scores, q_out, states)
        if nxt is not None:
            project(nxt, I)
        y = gated_norm(hp, o_heads)
        out_project(hp, y)
        if nxt is not None:
            project(nxt, G)

    xo = x + mod_ref[0, 2:3, :] * acc_ref[...]
    r = lax.rsqrt(jnp.mean(xo * xo, axis=-1, keepdims=True) + EPS)
    o_ref[0] = xo * r * fgain_ref[...]


def _hgrn2_layer(x, mod, gain, w, lb_raw, gn_gain, w_out, final_gain, layer):
    bsz, seq, d = x.shape
    tm = TOKENS_PER_STEP
    const = lambda nd: (lambda b, t: (0,) * nd)
    resident = functools.partial(pl.BlockSpec, pipeline_mode=pl.Buffered(1))
    return pl.pallas_call(
        functools.partial(_hgrn2_kernel, layer=layer),
        grid=(bsz, seq // tm),
        in_specs=[
            pl.BlockSpec((1, tm, d), lambda b, t: (b, t, 0)),
            pl.BlockSpec((1, 3, d), lambda b, t: (b, 0, 0)),
            resident((1, d), const(2)),
            resident(w.shape, const(3)),
            resident(lb_raw.shape, const(3)),
            resident((1, HG_HEAD_DIM), const(2)),
            resident(w_out.shape, const(3)),
            resident((1, d), const(2)),
        ],
        out_specs=pl.BlockSpec((1, tm, d), lambda b, t: (b, t, 0)),
        out_shape=jax.ShapeDtypeStruct(x.shape, x.dtype),
        scratch_shapes=[
            pltpu.VMEM((tm, d), _BF16),
            pltpu.VMEM((tm, d), _F32),
            pltpu.VMEM((HG_HEADS, HG_HEAD_DIM, HG_HEAD_DIM), _F32),
            pltpu.VMEM((2, 4, tm, PAIR_DIM), _F32),
        ],
        compiler_params=pltpu.CompilerParams(
            dimension_semantics=("arbitrary", "arbitrary"), vmem_limit_bytes=VMEM_LIMIT_BYTES),
        name="hgrn2_layer",
    )(x, mod, gain, w, lb_raw, gn_gain, w_out, final_gain)


def kernel(x, c, norm_gain, w_ada, b_ada, a_w_in, a_ln_gain, a_ln_bias, a_w_s, a_b_s, a_w_out,
           b_w_in, b_lower_bounds, b_gn_gain, b_w_out, final_gain):
    bsz, seq, d = x.shape
    depth = norm_gain.shape[0]
    assert depth == 2 and a_w_in.shape[0] == 1 and b_w_in.shape[0] == 1
    assert seq % TOKENS_PER_STEP == 0 and TOKENS_PER_STEP % SG_BLOCK == 0

    mod = _adaln_mod(c, w_ada, b_ada)
    mod = mod.reshape(depth, bsz, 3, d)

    wa = a_w_in[0].astype(_BF16)
    wu = wa[:, :D_INNER].reshape(d, SG_GROUPS, SG_GROUP_DIM)
    wv = wa[:, D_INNER:2 * D_INNER]
    wg = wa[:, 2 * D_INNER:].reshape(d, SG_GROUPS, SG_GROUP_DIM)
    wug = jnp.concatenate([wu, wg], axis=-1).transpose(1, 0, 2)
    wout_a = a_w_out[0].astype(_BF16).reshape(SG_GROUPS, SG_GROUP_DIM, d)
    x1 = _sgu_layer(
        x, mod[0], norm_gain[0:1], wv, wug, a_ln_gain, a_ln_bias, a_w_s[0],
        a_b_s[0].reshape(SG_GROUPS, SG_BLOCK, 1), wout_a)

    wb = b_w_in[0].astype(_BF16).reshape(d, 4, HEAD_PAIRS, PAIR_DIM)
    wb = wb.transpose(2, 0, 1, 3).reshape(HEAD_PAIRS, d, 4 * PAIR_DIM)
    lb_raw = b_lower_bounds.reshape(depth, HEAD_PAIRS, PAIR_DIM).transpose(1, 0, 2)
    wout_b = b_w_out[0].astype(_BF16).reshape(HEAD_PAIRS, PAIR_DIM, d)
    return _hgrn2_layer(
        x1, mod[1], norm_gain[1:2], wb, lb_raw, b_gn_gain, wout_b, final_gain.reshape(1, d),
        layer=1)
```

```python
import functools
import math

import jax
import jax.numpy as jnp
from jax import lax
from jax.experimental import pallas as pl
from jax.experimental.pallas import tpu as pltpu

D_MODEL = 1024
D_INNER = 2048
CHUNK = 64
SG_BLOCK = 128
SG_GROUPS = 8
SG_GROUP_DIM = D_INNER // SG_GROUPS
HG_HEADS = 16
HG_HEAD_DIM = D_INNER // HG_HEADS
HEAD_PAIRS = HG_HEADS // 2
PAIR_DIM = 2 * HG_HEAD_DIM
EPS = 1e-6

TOKENS_PER_STEP = 256
VMEM_LIMIT_BYTES = 56 * 1024 * 1024

_BF16 = jnp.bfloat16
_F32 = jnp.float32


def _dot(a, b):
    return jnp.dot(a, b, preferred_element_type=_F32)


def _dot_nt(a, b):
    return lax.dot_general(a, b, (((1,), (1,)), ((), ())), preferred_element_type=_F32)


def _dot_tn(a, b):
    return lax.dot_general(a, b, (((0,), (0,)), ((), ())), preferred_element_type=_F32)


def _sigmoid(z):
    return 1.0 / (1.0 + jnp.exp(-z))


def _silu(z):
    return z * _sigmoid(z)


def _gelu_tanh(z):
    c = math.sqrt(2.0 / math.pi)
    return 0.5 * z * (1.0 + jnp.tanh(c * (z + 0.044715 * (z * z * z))))


def _modulated_norm(x, gain, mod_ref):
    r = lax.rsqrt(jnp.mean(x * x, axis=-1, keepdims=True) + EPS)
    shift = mod_ref[0, 0:1, :]
    scale = mod_ref[0, 1:2, :]
    return (x * r * gain) * (1.0 + scale) + shift


def _adaln_kernel(c_ref, w_ref, b_ref, o_ref):
    c_act = _silu(c_ref[...]).astype(_BF16)
    o_ref[0] = _dot(c_act, w_ref[0].astype(_BF16)) + b_ref[0]


def _adaln_mod(c, w_ada, b_ada):
    depth, d, n3 = w_ada.shape
    bsz = c.shape[0]
    nt = n3 // d
    return pl.pallas_call(
        _adaln_kernel,
        grid=(depth, nt),
        in_specs=[
            pl.BlockSpec((bsz, d), lambda l, j: (0, 0)),
            pl.BlockSpec((1, d, d), lambda l, j: (l, 0, j)),
            pl.BlockSpec((1, 1, d), lambda l, j: (l, 0, j)),
        ],
        out_specs=pl.BlockSpec((1, bsz, d), lambda l, j: (l, 0, j)),
        out_shape=jax.ShapeDtypeStruct((depth, bsz, n3), _F32),
        compiler_params=pltpu.CompilerParams(dimension_semantics=("arbitrary", "arbitrary")),
        name="adaln_mod",
    )(c, w_ada, b_ada.reshape(depth, 1, n3))


def _sgu_kernel(x_ref, mod_ref, gain_ref, wv_ref, wug_ref, lng_ref, lnb_ref, ws_ref, bs_ref,
                wout_ref, o_ref, h_ref, v_ref, ug_ref, acc_ref):
    tm = x_ref.shape[1]
    x = x_ref[0]
    h_ref[...] = _modulated_norm(x, gain_ref[...], mod_ref).astype(_BF16)

    t_chunk = lax.broadcasted_iota(jnp.int32, (SG_BLOCK, SG_BLOCK), 0) // CHUNK
    s_chunk = lax.broadcasted_iota(jnp.int32, (SG_BLOCK, SG_BLOCK), 1) // CHUNK
    allowed = s_chunk <= t_chunk

    def group_cols(g):
        return slice(g * SG_GROUP_DIM, (g + 1) * SG_GROUP_DIM)

    def project_ug(g):
        ug_ref[g % 2] = _dot(h_ref[...], wug_ref[g])

    def lane_halves_sum(z):
        return z[:, :SG_GROUP_DIM // 2] + z[:, SG_GROUP_DIM // 2:]

    row_sum = None
    for g in range(SG_GROUPS):
        vg = _gelu_tanh(_dot(h_ref[...], wv_ref[:, group_cols(g)]))
        v_ref[g] = vg
        row_sum = lane_halves_sum(vg) if row_sum is None else row_sum + lane_halves_sum(vg)
    mu = jnp.sum(row_sum, axis=-1, keepdims=True) * (1.0 / D_INNER)
    project_ug(0)
    sq_sum = None
    for g in range(SG_GROUPS):
        vc = v_ref[g] - mu
        sq_sum = lane_halves_sum(vc * vc) if sq_sum is None else sq_sum + lane_halves_sum(vc * vc)
    var = jnp.sum(sq_sum, axis=-1, keepdims=True) * (1.0 / D_INNER)
    rstd = lax.rsqrt(var + EPS)

    for g in range(SG_GROUPS):
        vn = ((v_ref[g] - mu) * rstd * lng_ref[:, group_cols(g)] + lnb_ref[:, group_cols(g)]).astype(_BF16)
        ws = jnp.where(allowed, ws_ref[g], 0.0).astype(_BF16)
        bias = bs_ref[g]
        s = jnp.concatenate(
            [_dot(ws, vn[blk * SG_BLOCK:(blk + 1) * SG_BLOCK, :]) + bias for blk in range(tm // SG_BLOCK)],
            axis=0)
        if g + 1 < SG_GROUPS:
            project_ug(g + 1)
        u = _gelu_tanh(ug_ref[g % 2, :, :SG_GROUP_DIM])
        gate = _silu(ug_ref[g % 2, :, SG_GROUP_DIM:])
        y = (u * s * gate).astype(_BF16)
        contrib = _dot(y, wout_ref[g])
        if g == 0:
            acc_ref[...] = contrib
        else:
            acc_ref[...] += contrib

    o_ref[0] = x + mod_ref[0, 2:3, :] * acc_ref[...]


def _sgu_layer(x, mod, gain, wv, wug, ln_gain, ln_bias, w_s, b_s, w_out):
    bsz, seq, d = x.shape
    tm = TOKENS_PER_STEP
    const = lambda nd: (lambda b, t: (0,) * nd)
    resident = functools.partial(pl.BlockSpec, pipeline_mode=pl.Buffered(1))
    return pl.pallas_call(
        _sgu_kernel,
        grid=(bsz, seq // tm),
        in_specs=[
            pl.BlockSpec((1, tm, d), lambda b, t: (b, t, 0)),
            pl.BlockSpec((1, 3, d), lambda b, t: (b, 0, 0)),
            resident((1, d), const(2)),
            resident(wv.shape, const(2)),
            resident(wug.shape, const(3)),
            resident((1, D_INNER), const(2)),
            resident((1, D_INNER), const(2)),
            resident(w_s.shape, const(3)),
            resident(b_s.shape, const(3)),
            resident(w_out.shape, const(3)),
        ],
        out_specs=pl.BlockSpec((1, tm, d), lambda b, t: (b, t, 0)),
        out_shape=jax.ShapeDtypeStruct(x.shape, x.dtype),
        scratch_shapes=[
            pltpu.VMEM((tm, d), _BF16),
            pltpu.VMEM((SG_GROUPS, tm, SG_GROUP_DIM), _F32),
            pltpu.VMEM((2, tm, 2 * SG_GROUP_DIM), _F32),
            pltpu.VMEM((tm, d), _F32),
        ],
        compiler_params=pltpu.CompilerParams(
            dimension_semantics=("arbitrary", "arbitrary"), vmem_limit_bytes=VMEM_LIMIT_BYTES),
        name="sgu_layer",
    )(x, mod, gain, wv, wug, ln_gain, ln_bias, w_s, b_s, w_out)


def _hgrn2_kernel(x_ref, mod_ref, gain_ref, w_ref, lbraw_ref, gn_ref, wout_ref, fgain_ref,
                  o_ref, h_ref, acc_ref, state_ref, proj_ref, *, layer):
    tm = x_ref.shape[1]
    n_chunks = tm // CHUNK
    Q, F, I, G = range(4)

    @pl.when(pl.program_id(1) == 0)
    def _():
        state_ref[...] = jnp.zeros_like(state_ref)

    x = x_ref[0]
    h_ref[...] = _modulated_norm(x, gain_ref[...], mod_ref).astype(_BF16)

    row = lax.broadcasted_iota(jnp.int32, (tm, tm), 0)
    col = lax.broadcasted_iota(jnp.int32, (tm, tm), 1)
    intra = (row // CHUNK == col // CHUNK) & (col <= row)
    prefix = jnp.where(intra, 1.0, 0.0).astype(_BF16)
    gn = gn_ref[...]
    heads = [slice(hd * HG_HEAD_DIM, (hd + 1) * HG_HEAD_DIM) for hd in range(2)]
    chunks = [slice(c * CHUNK, (c + 1) * CHUNK) for c in range(n_chunks)]
    row_halves = [slice(0, tm // 2), slice(tm // 2, tm)]

    def project(hp, part):
        for rows in row_halves:
            proj_ref[hp % 2, part, rows] = _dot(h_ref[rows], w_ref[hp, part])

    def forget_gate(hp):
        lbraw = lbraw_ref[hp]
        e = jnp.exp(lbraw - jnp.max(lbraw, axis=0, keepdims=True))
        p = e / jnp.sum(e, axis=0, keepdims=True)
        lb = jnp.sum(p[1:layer + 1], axis=0, keepdims=True)
        f = lb + (1.0 - lb) * _sigmoid(proj_ref[hp % 2, F])
        log_f = jnp.log(f)
        lf_hi = log_f.astype(_BF16)
        r1 = log_f - lf_hi.astype(_F32)
        lf_mid = r1.astype(_BF16)
        lf_lo = (r1 - lf_mid.astype(_F32)).astype(_BF16)
        return 1.0 - f, (lf_hi, lf_mid, lf_lo)

    def prefix_sum(terms):
        return _dot(prefix, terms[0]) + _dot(prefix, terms[1]) + _dot(prefix, terms[2])

    def decay_scaling(hp, k, a):
        def chunk_rows(offset):
            return jnp.concatenate(
                [jnp.broadcast_to(a[c * CHUNK + offset:c * CHUNK + offset + 1, :], (CHUNK, PAIR_DIM))
                 for c in range(n_chunks)], axis=0)
        a_mid = chunk_rows(CHUNK // 2 - 1)
        a_end = chunk_rows(CHUNK - 1)
        q = _silu(proj_ref[hp % 2, Q])
        q_in = q * jnp.exp(a - a_mid)
        k_in = k * jnp.exp(a_mid - a)
        q_out = (q_in * jnp.exp(a_mid)).astype(_BF16)
        k_out = (k_in * jnp.exp(a_end - a_mid)).astype(_BF16)
        decay = [jnp.exp(a[c * CHUNK + CHUNK - 1:c * CHUNK + CHUNK, :]) for c in range(n_chunks)]
        return q_in.astype(_BF16), k_in.astype(_BF16), q_out, k_out, decay

    def scores_and_updates(hp, q_in, k_in, k_out):
        vv = proj_ref[hp % 2, I].astype(_BF16)
        scores = [_dot_nt(q_in[:, ln], k_in[:, ln]) for ln in heads]
        kv = [[_dot_tn(vv[rw, ln], k_out[rw, ln]) for rw in chunks] for ln in heads]
        return vv, scores, kv

    def mask_and_states(hp, scores, kv, decay):
        scores = [jnp.where(intra, s, 0.0).astype(_BF16) for s in scores]
        states = []
        for hd, ln in enumerate(heads):
            st = state_ref[hp * 2 + hd]
            per_chunk = []
            for c in range(n_chunks):
                per_chunk.append(st.astype(_BF16))
                st = decay[c][:, ln] * st + kv[hd][c]
            state_ref[hp * 2 + hd] = st
            states.append(per_chunk)
        return scores, states

    def outputs(vv, scores, q_out, states):
        o_heads = []
        for hd, ln in enumerate(heads):
            o_inter = [_dot_nt(q_out[rw, ln], states[hd][c]) for c, rw in enumerate(chunks)]
            o_heads.append(_dot(scores[hd], vv[:, ln]) + jnp.concatenate(o_inter, axis=0))
        return o_heads

    def gated_norm(hp, o_heads):
        og = _silu(proj_ref[hp % 2, G])
        normed = [o * lax.rsqrt(jnp.mean(o * o, axis=-1, keepdims=True) + EPS) * gn for o in o_heads]
        return (jnp.concatenate(normed, axis=1) * og).astype(_BF16)

    def out_project(hp, y):
        for n in range(D_MODEL // PAIR_DIM):
            cols = slice(n * PAIR_DIM, (n + 1) * PAIR_DIM)
            contrib = _dot(y, wout_ref[hp, n])
            if hp == 0:
                acc_ref[:, cols] = contrib
            else:
                acc_ref[:, cols] += contrib

    for part in (F, Q, I, G):
        project(0, part)
    y_prev = None
    for hp in range(HEAD_PAIRS):
        nxt = hp + 1 if hp + 1 < HEAD_PAIRS else None
        k, terms = forget_gate(hp)
        a = prefix_sum(terms)
        if y_prev is not None:
            out_project(hp - 1, y_prev)
        if nxt is not None:
            project(nxt, F)
        q_in, k_in, q_out, k_out, decay = decay_scaling(hp, k, a)
        vv, scores, kv = scores_and_updates(hp, q_in, k_in, k_out)
        if nxt is not None:
            project(nxt, Q)
        scores, states = mask_and_states(hp, scores, kv, decay)
        o_heads = outputs(vv, scores, q_out, states)
        if nxt is not None:
            project(nxt, I)
            project(nxt, G)
        y_prev = gated_norm(hp, o_heads)
    out_project(HEAD_PAIRS - 1, y_prev)

    xo = x + mod_ref[0, 2:3, :] * acc_ref[...]
    r = lax.rsqrt(jnp.mean(xo * xo, axis=-1, keepdims=True) + EPS)
    o_ref[0] = xo * r * fgain_ref[...]


def _hgrn2_layer(x, mod, gain, w, lb_raw, gn_gain, w_out, final_gain, layer):
    bsz, seq, d = x.shape
    tm = TOKENS_PER_STEP
    const = lambda nd: (lambda b, t: (0,) * nd)
    resident = functools.partial(pl.BlockSpec, pipeline_mode=pl.Buffered(1))
    return pl.pallas_call(
        functools.partial(_hgrn2_kernel, layer=layer),
        grid=(bsz, seq // tm),
        in_specs=[
            pl.BlockSpec((1, tm, d), lambda b, t: (b, t, 0)),
            pl.BlockSpec((1, 3, d), lambda b, t: (b, 0, 0)),
            resident((1, d), const(2)),
            resident(w.shape, const(4)),
            resident(lb_raw.shape, const(3)),
            resident((1, HG_HEAD_DIM), const(2)),
            resident(w_out.shape, const(4)),
            resident((1, d), const(2)),
        ],
        out_specs=pl.BlockSpec((1, tm, d), lambda b, t: (b, t, 0)),
        out_shape=jax.ShapeDtypeStruct(x.shape, x.dtype),
        scratch_shapes=[
            pltpu.VMEM((tm, d), _BF16),
            pltpu.VMEM((tm, d), _F32),
            pltpu.VMEM((HG_HEADS, HG_HEAD_DIM, HG_HEAD_DIM), _F32),
            pltpu.VMEM((2, 4, tm, PAIR_DIM), _F32),
        ],
        compiler_params=pltpu.CompilerParams(
            dimension_semantics=("arbitrary", "arbitrary"), vmem_limit_bytes=VMEM_LIMIT_BYTES),
        name="hgrn2_layer",
    )(x, mod, gain, w, lb_raw, gn_gain, w_out, final_gain)


def kernel(x, c, norm_gain, w_ada, b_ada, a_w_in, a_ln_gain, a_ln_bias, a_w_s, a_b_s, a_w_out,
           b_w_in, b_lower_bounds, b_gn_gain, b_w_out, final_gain):
    bsz, seq, d = x.shape
    depth = norm_gain.shape[0]
    assert depth == 2 and a_w_in.shape[0] == 1 and b_w_in.shape[0] == 1
    assert seq % TOKENS_PER_STEP == 0 and TOKENS_PER_STEP % SG_BLOCK == 0

    mod = _adaln_mod(c, w_ada, b_ada)
    mod = mod.reshape(depth, bsz, 3, d)

    wa = a_w_in[0].astype(_BF16)
    wu = wa[:, :D_INNER].reshape(d, SG_GROUPS, SG_GROUP_DIM)
    wv = wa[:, D_INNER:2 * D_INNER]
    wg = wa[:, 2 * D_INNER:].reshape(d, SG_GROUPS, SG_GROUP_DIM)
    wug = jnp.concatenate([wu, wg], axis=-1).transpose(1, 0, 2)
    wout_a = a_w_out[0].astype(_BF16).reshape(SG_GROUPS, SG_GROUP_DIM, d)
    x1 = _sgu_layer(
        x, mod[0], norm_gain[0:1], wv, wug, a_ln_gain, a_ln_bias, a_w_s[0],
        a_b_s[0].reshape(SG_GROUPS, SG_BLOCK, 1), wout_a)

    wb = b_w_in[0].astype(_BF16).reshape(d, 4, HEAD_PAIRS, PAIR_DIM)
    wb = wb.transpose(2, 1, 0, 3)
    lb_raw = b_lower_bounds.reshape(depth, HEAD_PAIRS, PAIR_DIM).transpose(1, 0, 2)
    wout_b = b_w_out[0].astype(_BF16).reshape(HEAD_PAIRS, PAIR_DIM, d // PAIR_DIM, PAIR_DIM)
    wout_b = wout_b.transpose(0, 2, 1, 3)
    return _hgrn2_layer(
        x1, mod[1], norm_gain[1:2], wb, lb_raw, b_gn_gain, wout_b, final_gain.reshape(1, d),
        layer=1)
```

```python
import functools
import math

import jax
import jax.numpy as jnp
from jax import lax
from jax.experimental import pallas as pl
from jax.experimental.pallas import tpu as pltpu

D_MODEL = 1024
D_INNER = 2048
CHUNK = 64
SG_BLOCK = 128
SG_GROUPS = 8
SG_GROUP_DIM = D_INNER // SG_GROUPS
HG_HEADS = 16
HG_HEAD_DIM = D_INNER // HG_HEADS
HEAD_PAIRS = HG_HEADS // 2
PAIR_DIM = 2 * HG_HEAD_DIM
EPS = 1e-6

TOKENS_PER_STEP = 256
VMEM_LIMIT_BYTES = 56 * 1024 * 1024

_BF16 = jnp.bfloat16
_F32 = jnp.float32


def _dot(a, b):
    return jnp.dot(a, b, preferred_element_type=_F32)


def _dot_nt(a, b):
    return lax.dot_general(a, b, (((1,), (1,)), ((), ())), preferred_element_type=_F32)


def _dot_tn(a, b):
    return lax.dot_general(a, b, (((0,), (0,)), ((), ())), preferred_element_type=_F32)


def _sigmoid(z):
    return 1.0 / (1.0 + jnp.exp(-z))


def _silu(z):
    return z * _sigmoid(z)


def _gelu_tanh(z):
    c = math.sqrt(2.0 / math.pi)
    return 0.5 * z * (1.0 + jnp.tanh(c * (z + 0.044715 * (z * z * z))))


def _modulated_norm(x, gain, mod_ref):
    r = lax.rsqrt(jnp.mean(x * x, axis=-1, keepdims=True) + EPS)
    shift = mod_ref[0, 0:1, :]
    scale = mod_ref[0, 1:2, :]
    return (x * r * gain) * (1.0 + scale) + shift


def _adaln_kernel(c_ref, w_ref, b_ref, o_ref):
    c_act = _silu(c_ref[...]).astype(_BF16)
    o_ref[0] = _dot(c_act, w_ref[0].astype(_BF16)) + b_ref[0]


def _adaln_mod(c, w_ada, b_ada):
    depth, d, n3 = w_ada.shape
    bsz = c.shape[0]
    nt = n3 // d
    return pl.pallas_call(
        _adaln_kernel,
        grid=(depth, nt),
        in_specs=[
            pl.BlockSpec((bsz, d), lambda l, j: (0, 0)),
            pl.BlockSpec((1, d, d), lambda l, j: (l, 0, j)),
            pl.BlockSpec((1, 1, d), lambda l, j: (l, 0, j)),
        ],
        out_specs=pl.BlockSpec((1, bsz, d), lambda l, j: (l, 0, j)),
        out_shape=jax.ShapeDtypeStruct((depth, bsz, n3), _F32),
        compiler_params=pltpu.CompilerParams(dimension_semantics=("arbitrary", "arbitrary")),
        name="adaln_mod",
    )(c, w_ada, b_ada.reshape(depth, 1, n3))


def _sgu_kernel(x_ref, mod_ref, gain_ref, wv_ref, wug_ref, lng_ref, lnb_ref, ws_ref, bs_ref,
                wout_ref, o_ref, h_ref, v_ref, ug_ref, acc_ref):
    tm = x_ref.shape[1]
    x = x_ref[0]
    h_ref[...] = _modulated_norm(x, gain_ref[...], mod_ref).astype(_BF16)

    t_chunk = lax.broadcasted_iota(jnp.int32, (SG_BLOCK, SG_BLOCK), 0) // CHUNK
    s_chunk = lax.broadcasted_iota(jnp.int32, (SG_BLOCK, SG_BLOCK), 1) // CHUNK
    allowed = s_chunk <= t_chunk

    def group_cols(g):
        return slice(g * SG_GROUP_DIM, (g + 1) * SG_GROUP_DIM)

    def project_ug(g):
        ug_ref[g % 2] = _dot(h_ref[...], wug_ref[g])

    def lane_halves_sum(z):
        return z[:, :SG_GROUP_DIM // 2] + z[:, SG_GROUP_DIM // 2:]

    row_sum = None
    for g in range(SG_GROUPS):
        vg = _gelu_tanh(_dot(h_ref[...], wv_ref[:, group_cols(g)]))
        v_ref[g] = vg
        row_sum = lane_halves_sum(vg) if row_sum is None else row_sum + lane_halves_sum(vg)
    mu = jnp.sum(row_sum, axis=-1, keepdims=True) * (1.0 / D_INNER)
    project_ug(0)
    sq_sum = None
    for g in range(SG_GROUPS):
        vc = v_ref[g] - mu
        sq_sum = lane_halves_sum(vc * vc) if sq_sum is None else sq_sum + lane_halves_sum(vc * vc)
    var = jnp.sum(sq_sum, axis=-1, keepdims=True) * (1.0 / D_INNER)
    rstd = lax.rsqrt(var + EPS)

    for g in range(SG_GROUPS):
        vn = ((v_ref[g] - mu) * rstd * lng_ref[:, group_cols(g)] + lnb_ref[:, group_cols(g)]).astype(_BF16)
        ws = jnp.where(allowed, ws_ref[g], 0.0).astype(_BF16)
        bias = bs_ref[g]
        s = jnp.concatenate(
            [_dot(ws, vn[blk * SG_BLOCK:(blk + 1) * SG_BLOCK, :]) + bias for blk in range(tm // SG_BLOCK)],
            axis=0)
        if g + 1 < SG_GROUPS:
            project_ug(g + 1)
        u = _gelu_tanh(ug_ref[g % 2, :, :SG_GROUP_DIM])
        gate = _silu(ug_ref[g % 2, :, SG_GROUP_DIM:])
        y = (u * s * gate).astype(_BF16)
        contrib = _dot(y, wout_ref[g])
        if g == 0:
            acc_ref[...] = contrib
        else:
            acc_ref[...] += contrib

    o_ref[0] = x + mod_ref[0, 2:3, :] * acc_ref[...]


def _sgu_layer(x, mod, gain, wv, wug, ln_gain, ln_bias, w_s, b_s, w_out):
    bsz, seq, d = x.shape
    tm = TOKENS_PER_STEP
    const = lambda nd: (lambda b, t: (0,) * nd)
    resident = functools.partial(pl.BlockSpec, pipeline_mode=pl.Buffered(1))
    return pl.pallas_call(
        _sgu_kernel,
        grid=(bsz, seq // tm),
        in_specs=[
            pl.BlockSpec((1, tm, d), lambda b, t: (b, t, 0)),
            pl.BlockSpec((1, 3, d), lambda b, t: (b, 0, 0)),
            resident((1, d), const(2)),
            resident(wv.shape, const(2)),
            resident(wug.shape, const(3)),
            resident((1, D_INNER), const(2)),
            resident((1, D_INNER), const(2)),
            resident(w_s.shape, const(3)),
            resident(b_s.shape, const(3)),
            resident(w_out.shape, const(3)),
        ],
        out_specs=pl.BlockSpec((1, tm, d), lambda b, t: (b, t, 0)),
        out_shape=jax.ShapeDtypeStruct(x.shape, x.dtype),
        scratch_shapes=[
            pltpu.VMEM((tm, d), _BF16),
            pltpu.VMEM((SG_GROUPS, tm, SG_GROUP_DIM), _F32),
            pltpu.VMEM((2, tm, 2 * SG_GROUP_DIM), _F32),
            pltpu.VMEM((tm, d), _F32),
        ],
        compiler_params=pltpu.CompilerParams(
            dimension_semantics=("arbitrary", "arbitrary"), vmem_limit_bytes=VMEM_LIMIT_BYTES),
        name="sgu_layer",
    )(x, mod, gain, wv, wug, ln_gain, ln_bias, w_s, b_s, w_out)


def _hgrn2_kernel(x_ref, mod_ref, gain_ref, w_ref, lbraw_ref, gn_ref, wout_ref, fgain_ref,
                  o_ref, h_ref, acc_ref, state_ref, proj_ref, *, layer):
    tm = x_ref.shape[1]
    n_chunks = tm // CHUNK
    Q, F, I, G = range(4)

    @pl.when(pl.program_id(1) == 0)
    def _():
        state_ref[...] = jnp.zeros_like(state_ref)

    x = x_ref[0]
    h_ref[...] = _modulated_norm(x, gain_ref[...], mod_ref).astype(_BF16)

    row = lax.broadcasted_iota(jnp.int32, (tm, tm), 0)
    col = lax.broadcasted_iota(jnp.int32, (tm, tm), 1)
    intra = (row // CHUNK == col // CHUNK) & (col <= row)
    prefix = jnp.where(intra, 1.0, 0.0).astype(_BF16)
    gn = gn_ref[...]
    heads = [slice(hd * HG_HEAD_DIM, (hd + 1) * HG_HEAD_DIM) for hd in range(2)]
    chunks = [slice(c * CHUNK, (c + 1) * CHUNK) for c in range(n_chunks)]
    row_halves = [slice(0, tm // 2), slice(tm // 2, tm)]

    def project(hp, part):
        for rows in row_halves:
            proj_ref[hp % 2, part, rows] = _dot(h_ref[rows], w_ref[hp, part])

    def forget_gate(hp):
        lbraw = lbraw_ref[hp]
        e = jnp.exp(lbraw - jnp.max(lbraw, axis=0, keepdims=True))
        p = e / jnp.sum(e, axis=0, keepdims=True)
        lb = jnp.sum(p[1:layer + 1], axis=0, keepdims=True)
        f = lb + (1.0 - lb) * _sigmoid(proj_ref[hp % 2, F])
        log_f = jnp.log(f)
        lf_hi = log_f.astype(_BF16)
        r1 = log_f - lf_hi.astype(_F32)
        lf_mid = r1.astype(_BF16)
        lf_lo = (r1 - lf_mid.astype(_F32)).astype(_BF16)
        return 1.0 - f, (lf_hi, lf_mid, lf_lo)

    def prefix_sum(terms):
        return _dot(prefix, terms[0]) + _dot(prefix, terms[1]) + _dot(prefix, terms[2])

    def decay_scaling(hp, k, a):
        def chunk_rows(offset):
            return jnp.concatenate(
                [jnp.broadcast_to(a[c * CHUNK + offset:c * CHUNK + offset + 1, :], (CHUNK, PAIR_DIM))
                 for c in range(n_chunks)], axis=0)
        a_mid = chunk_rows(CHUNK // 2 - 1)
        a_end = chunk_rows(CHUNK - 1)
        q = _silu(proj_ref[hp % 2, Q])
        q_in = q * jnp.exp(a - a_mid)
        k_in = k * jnp.exp(a_mid - a)
        q_out = (q_in * jnp.exp(a_mid)).astype(_BF16)
        k_out = (k_in * jnp.exp(a_end - a_mid)).astype(_BF16)
        decay = [jnp.exp(a[c * CHUNK + CHUNK - 1:c * CHUNK + CHUNK, :]) for c in range(n_chunks)]
        k_in_t = k_in.T.astype(_BF16)
        return q_in.astype(_BF16), k_in_t, q_out, k_out, decay

    def scores_and_updates(hp, q_in, k_in_t, k_out):
        vv32 = proj_ref[hp % 2, I]
        vv = vv32.astype(_BF16)
        vv_t = vv32.T.astype(_BF16)
        scores = [_dot(q_in[:, ln], k_in_t[ln, :]) for ln in heads]
        kv = []
        for ln in heads:
            blocks = []
            for c, rw in enumerate(chunks):
                pieces = [k_out[rw, ln]]
                if c > 0:
                    pieces.insert(0, jnp.zeros((c * CHUNK, HG_HEAD_DIM), _BF16))
                if c + 1 < n_chunks:
                    pieces.append(jnp.zeros(((n_chunks - 1 - c) * CHUNK, HG_HEAD_DIM), _BF16))
                blocks.append(jnp.concatenate(pieces, axis=0))
            kv.append(_dot(vv_t[ln, :], jnp.concatenate(blocks, axis=1)))
        return vv, scores, kv

    def mask_and_states(hp, scores, kv, decay):
        scores = [jnp.where(intra, s, 0.0).astype(_BF16) for s in scores]
        states = []
        for hd, ln in enumerate(heads):
            st = state_ref[hp * 2 + hd]
            per_chunk = []
            for c in range(n_chunks):
                per_chunk.append(st.T.astype(_BF16))
                st = decay[c][:, ln] * st + kv[hd][:, c * HG_HEAD_DIM:(c + 1) * HG_HEAD_DIM]
            state_ref[hp * 2 + hd] = st
            states.append(per_chunk)
        return scores, states

    def outputs(vv, scores, q_out, states):
        o_heads = []
        for hd, ln in enumerate(heads):
            o_chunks = []
            for c, rw in enumerate(chunks):
                blk = slice((c // 2) * SG_BLOCK, (c // 2 + 1) * SG_BLOCK)
                lhs = jnp.concatenate([q_out[rw, ln], scores[hd][rw, blk]], axis=1)
                rhs = jnp.concatenate([states[hd][c], vv[blk, ln]], axis=0)
                o_chunks.append(_dot(lhs, rhs))
            o_heads.append(jnp.concatenate(o_chunks, axis=0))
        return o_heads

    def gated_norm(hp, o_heads):
        og = _silu(proj_ref[hp % 2, G])
        normed = [o * lax.rsqrt(jnp.mean(o * o, axis=-1, keepdims=True) + EPS) * gn for o in o_heads]
        return (jnp.concatenate(normed, axis=1) * og).astype(_BF16)

    def out_project(hp, y):
        for n in range(D_MODEL // PAIR_DIM):
            cols = slice(n * PAIR_DIM, (n + 1) * PAIR_DIM)
            contrib = _dot(y, wout_ref[hp, n])
            if hp == 0:
                acc_ref[:, cols] = contrib
            else:
                acc_ref[:, cols] += contrib

    for part in (F, Q, I, G):
        project(0, part)
    y_prev = None
    for hp in range(HEAD_PAIRS):
        nxt = hp + 1 if hp + 1 < HEAD_PAIRS else None
        k, terms = forget_gate(hp)
        a = prefix_sum(terms)
        if y_prev is not None:
            out_project(hp - 1, y_prev)
        if nxt is not None:
            project(nxt, F)
        q_in, k_in_t, q_out, k_out, decay = decay_scaling(hp, k, a)
        vv, scores, kv = scores_and_updates(hp, q_in, k_in_t, k_out)
        if nxt is not None:
            project(nxt, Q)
        scores, states = mask_and_states(hp, scores, kv, decay)
        o_heads = outputs(vv, scores, q_out, states)
        if nxt is not None:
            project(nxt, I)
            project(nxt, G)
        y_prev = gated_norm(hp, o_heads)
    out_project(HEAD_PAIRS - 1, y_prev)

    xo = x + mod_ref[0, 2:3, :] * acc_ref[...]
    r = lax.rsqrt(jnp.mean(xo * xo, axis=-1, keepdims=True) + EPS)
    o_ref[0] = xo * r * fgain_ref[...]


def _hgrn2_layer(x, mod, gain, w, lb_raw, gn_gain, w_out, final_gain, layer):
    bsz, seq, d = x.shape
    tm = TOKENS_PER_STEP
    const = lambda nd: (lambda b, t: (0,) * nd)
    resident = functools.partial(pl.BlockSpec, pipeline_mode=pl.Buffered(1))
    return pl.pallas_call(
        functools.partial(_hgrn2_kernel, layer=layer),
        grid=(bsz, seq // tm),
        in_specs=[
            pl.BlockSpec((1, tm, d), lambda b, t: (b, t, 0)),
            pl.BlockSpec((1, 3, d), lambda b, t: (b, 0, 0)),
            resident((1, d), const(2)),
            resident(w.shape, const(4)),
            resident(lb_raw.shape, const(3)),
            resident((1, HG_HEAD_DIM), const(2)),
            resident(w_out.shape, const(4)),
            resident((1, d), const(2)),
        ],
        out_specs=pl.BlockSpec((1, tm, d), lambda b, t: (b, t, 0)),
        out_shape=jax.ShapeDtypeStruct(x.shape, x.dtype),
        scratch_shapes=[
            pltpu.VMEM((tm, d), _BF16),
            pltpu.VMEM((tm, d), _F32),
            pltpu.VMEM((HG_HEADS, HG_HEAD_DIM, HG_HEAD_DIM), _F32),
            pltpu.VMEM((2, 4, tm, PAIR_DIM), _F32),
        ],
        compiler_params=pltpu.CompilerParams(
            dimension_semantics=("arbitrary", "arbitrary"), vmem_limit_bytes=VMEM_LIMIT_BYTES),
        name="hgrn2_layer",
    )(x, mod, gain, w, lb_raw, gn_gain, w_out, final_gain)


def kernel(x, c, norm_gain, w_ada, b_ada, a_w_in, a_ln_gain, a_ln_bias, a_w_s, a_b_s, a_w_out,
           b_w_in, b_lower_bounds, b_gn_gain, b_w_out, final_gain):
    bsz, seq, d = x.shape
    depth = norm_gain.shape[0]
    assert depth == 2 and a_w_in.shape[0] == 1 and b_w_in.shape[0] == 1
    assert seq % TOKENS_PER_STEP == 0 and TOKENS_PER_STEP % SG_BLOCK == 0

    mod = _adaln_mod(c, w_ada, b_ada)
    mod = mod.reshape(depth, bsz, 3, d)

    wa = a_w_in[0].astype(_BF16)
    wu = wa[:, :D_INNER].reshape(d, SG_GROUPS, SG_GROUP_DIM)
    wv = wa[:, D_INNER:2 * D_INNER]
    wg = wa[:, 2 * D_INNER:].reshape(d, SG_GROUPS, SG_GROUP_DIM)
    wug = jnp.concatenate([wu, wg], axis=-1).transpose(1, 0, 2)
    wout_a = a_w_out[0].astype(_BF16).reshape(SG_GROUPS, SG_GROUP_DIM, d)
    x1 = _sgu_layer(
        x, mod[0], norm_gain[0:1], wv, wug, a_ln_gain, a_ln_bias, a_w_s[0],
        a_b_s[0].reshape(SG_GROUPS, SG_BLOCK, 1), wout_a)

    wb = b_w_in[0].astype(_BF16).reshape(d, 4, HEAD_PAIRS, PAIR_DIM)
    wb = wb.transpose(2, 1, 0, 3)
    lb_raw = b_lower_bounds.reshape(depth, HEAD_PAIRS, PAIR_DIM).transpose(1, 0, 2)
    wout_b = b_w_out[0].astype(_BF16).reshape(HEAD_PAIRS, PAIR_DIM, d // PAIR_DIM, PAIR_DIM)
    wout_b = wout_b.transpose(0, 2, 1, 3)
    return _hgrn2_layer(
        x1, mod[1], norm_gain[1:2], wb, lb_raw, b_gn_gain, wout_b, final_gain.reshape(1, d),
        layer=1)
```

```python
import functools
import math

import jax
import jax.numpy as jnp
from jax import lax
from jax.experimental import pallas as pl
from jax.experimental.pallas import tpu as pltpu

D_MODEL = 1024
D_INNER = 2048
CHUNK = 64
SG_BLOCK = 128
SG_GROUPS = 8
SG_GROUP_DIM = D_INNER // SG_GROUPS
HG_HEADS = 16
HG_HEAD_DIM = D_INNER // HG_HEADS
HEAD_PAIRS = HG_HEADS // 2
PAIR_DIM = 2 * HG_HEAD_DIM
EPS = 1e-6

TOKENS_PER_STEP = 256
VMEM_LIMIT_BYTES = 56 * 1024 * 1024

_BF16 = jnp.bfloat16
_F32 = jnp.float32


def _dot(a, b):
    return jnp.dot(a, b, preferred_element_type=_F32)


def _dot_nt(a, b):
    return lax.dot_general(a, b, (((1,), (1,)), ((), ())), preferred_element_type=_F32)


def _dot_tn(a, b):
    return lax.dot_general(a, b, (((0,), (0,)), ((), ())), preferred_element_type=_F32)


def _sigmoid(z):
    return 1.0 / (1.0 + jnp.exp(-z))


def _silu(z):
    return z * _sigmoid(z)


def _gelu_tanh(z):
    c = math.sqrt(2.0 / math.pi)
    return 0.5 * z * (1.0 + jnp.tanh(c * (z + 0.044715 * (z * z * z))))


def _modulated_norm(x, gain, mod_ref):
    r = lax.rsqrt(jnp.mean(x * x, axis=-1, keepdims=True) + EPS)
    shift = mod_ref[0, 0:1, :]
    scale = mod_ref[0, 1:2, :]
    return (x * r * gain) * (1.0 + scale) + shift


def _adaln_kernel(c_ref, w_ref, b_ref, o_ref):
    c_act = _silu(c_ref[...]).astype(_BF16)
    o_ref[0] = _dot(c_act, w_ref[0].astype(_BF16)) + b_ref[0]


def _adaln_mod(c, w_ada, b_ada):
    depth, d, n3 = w_ada.shape
    bsz = c.shape[0]
    nt = n3 // d
    return pl.pallas_call(
        _adaln_kernel,
        grid=(depth, nt),
        in_specs=[
            pl.BlockSpec((bsz, d), lambda l, j: (0, 0)),
            pl.BlockSpec((1, d, d), lambda l, j: (l, 0, j)),
            pl.BlockSpec((1, 1, d), lambda l, j: (l, 0, j)),
        ],
        out_specs=pl.BlockSpec((1, bsz, d), lambda l, j: (l, 0, j)),
        out_shape=jax.ShapeDtypeStruct((depth, bsz, n3), _F32),
        compiler_params=pltpu.CompilerParams(dimension_semantics=("arbitrary", "arbitrary")),
        name="adaln_mod",
    )(c, w_ada, b_ada.reshape(depth, 1, n3))


def _sgu_kernel(x_ref, mod_ref, gain_ref, wv_ref, wug_ref, lng_ref, lnb_ref, ws_ref, bs_ref,
                wout_ref, o_ref, h_ref, v_ref, ug_ref, acc_ref):
    tm = x_ref.shape[1]
    x = x_ref[0]
    h_ref[...] = _modulated_norm(x, gain_ref[...], mod_ref).astype(_BF16)

    t_chunk = lax.broadcasted_iota(jnp.int32, (SG_BLOCK, SG_BLOCK), 0) // CHUNK
    s_chunk = lax.broadcasted_iota(jnp.int32, (SG_BLOCK, SG_BLOCK), 1) // CHUNK
    allowed = s_chunk <= t_chunk

    def group_cols(g):
        return slice(g * SG_GROUP_DIM, (g + 1) * SG_GROUP_DIM)

    def project_ug(g):
        ug_ref[g % 2] = _dot(h_ref[...], wug_ref[g])

    def lane_halves_sum(z):
        return z[:, :SG_GROUP_DIM // 2] + z[:, SG_GROUP_DIM // 2:]

    row_sum = None
    for g in range(SG_GROUPS):
        vg = _gelu_tanh(_dot(h_ref[...], wv_ref[:, group_cols(g)]))
        v_ref[g] = vg
        row_sum = lane_halves_sum(vg) if row_sum is None else row_sum + lane_halves_sum(vg)
    mu = jnp.sum(row_sum, axis=-1, keepdims=True) * (1.0 / D_INNER)
    project_ug(0)
    sq_sum = None
    for g in range(SG_GROUPS):
        vc = v_ref[g] - mu
        sq_sum = lane_halves_sum(vc * vc) if sq_sum is None else sq_sum + lane_halves_sum(vc * vc)
    var = jnp.sum(sq_sum, axis=-1, keepdims=True) * (1.0 / D_INNER)
    rstd = lax.rsqrt(var + EPS)

    for g in range(SG_GROUPS):
        vn = ((v_ref[g] - mu) * rstd * lng_ref[:, group_cols(g)] + lnb_ref[:, group_cols(g)]).astype(_BF16)
        ws = jnp.where(allowed, ws_ref[g], 0.0).astype(_BF16)
        bias = bs_ref[g]
        s = jnp.concatenate(
            [_dot(ws, vn[blk * SG_BLOCK:(blk + 1) * SG_BLOCK, :]) + bias for blk in range(tm // SG_BLOCK)],
            axis=0)
        if g + 1 < SG_GROUPS:
            project_ug(g + 1)
        u = _gelu_tanh(ug_ref[g % 2, :, :SG_GROUP_DIM])
        gate = _silu(ug_ref[g % 2, :, SG_GROUP_DIM:])
        y = (u * s * gate).astype(_BF16)
        contrib = _dot(y, wout_ref[g])
        if g == 0:
            acc_ref[...] = contrib
        else:
            acc_ref[...] += contrib

    o_ref[0] = x + mod_ref[0, 2:3, :] * acc_ref[...]


def _sgu_layer(x, mod, gain, wv, wug, ln_gain, ln_bias, w_s, b_s, w_out):
    bsz, seq, d = x.shape
    tm = TOKENS_PER_STEP
    const = lambda nd: (lambda b, t: (0,) * nd)
    resident = functools.partial(pl.BlockSpec, pipeline_mode=pl.Buffered(1))
    return pl.pallas_call(
        _sgu_kernel,
        grid=(bsz, seq // tm),
        in_specs=[
            pl.BlockSpec((1, tm, d), lambda b, t: (b, t, 0)),
            pl.BlockSpec((1, 3, d), lambda b, t: (b, 0, 0)),
            resident((1, d), const(2)),
            resident(wv.shape, const(2)),
            resident(wug.shape, const(3)),
            resident((1, D_INNER), const(2)),
            resident((1, D_INNER), const(2)),
            resident(w_s.shape, const(3)),
            resident(b_s.shape, const(3)),
            resident(w_out.shape, const(3)),
        ],
        out_specs=pl.BlockSpec((1, tm, d), lambda b, t: (b, t, 0)),
        out_shape=jax.ShapeDtypeStruct(x.shape, x.dtype),
        scratch_shapes=[
            pltpu.VMEM((tm, d), _BF16),
            pltpu.VMEM((SG_GROUPS, tm, SG_GROUP_DIM), _F32),
            pltpu.VMEM((2, tm, 2 * SG_GROUP_DIM), _F32),
            pltpu.VMEM((tm, d), _F32),
        ],
        compiler_params=pltpu.CompilerParams(
            dimension_semantics=("arbitrary", "arbitrary"), vmem_limit_bytes=VMEM_LIMIT_BYTES),
        name="sgu_layer",
    )(x, mod, gain, wv, wug, ln_gain, ln_bias, w_s, b_s, w_out)


def _hgrn2_kernel(x_ref, mod_ref, gain_ref, w_ref, lbraw_ref, gn_ref, wout_ref, fgain_ref,
                  o_ref, h_ref, acc_ref, state_ref, proj_ref, *, layer):
    tm = x_ref.shape[1]
    n_chunks = tm // CHUNK
    Q, F, I, G = range(4)

    @pl.when(pl.program_id(1) == 0)
    def _():
        state_ref[...] = jnp.zeros_like(state_ref)

    x = x_ref[0]
    h_ref[...] = _modulated_norm(x, gain_ref[...], mod_ref).astype(_BF16)

    row = lax.broadcasted_iota(jnp.int32, (tm, tm), 0)
    col = lax.broadcasted_iota(jnp.int32, (tm, tm), 1)
    intra = (row // CHUNK == col // CHUNK) & (col <= row)
    step_in_chunk = lax.broadcasted_iota(jnp.int32, (tm, PAIR_DIM), 0) % CHUNK
    gn = gn_ref[...]
    heads = [slice(hd * HG_HEAD_DIM, (hd + 1) * HG_HEAD_DIM) for hd in range(2)]
    chunks = [slice(c * CHUNK, (c + 1) * CHUNK) for c in range(n_chunks)]
    row_halves = [slice(0, tm // 2), slice(tm // 2, tm)]

    def project(hp, part):
        for rows in row_halves:
            proj_ref[hp % 2, part, rows] = _dot(h_ref[rows], w_ref[hp, part])

    def forget_gate(hp):
        lbraw = lbraw_ref[hp]
        e = jnp.exp(lbraw - jnp.max(lbraw, axis=0, keepdims=True))
        p = e / jnp.sum(e, axis=0, keepdims=True)
        lb = jnp.sum(p[1:layer + 1], axis=0, keepdims=True)
        f = lb + (1.0 - lb) * _sigmoid(proj_ref[hp % 2, F])
        return 1.0 - f, jnp.log(f)

    def prefix_sum(z):
        shift = 1
        while shift < CHUNK:
            z = z + jnp.where(step_in_chunk >= shift, pltpu.roll(z, shift, axis=0), 0.0)
            shift *= 2
        return z

    def decay_scaling(hp, k, a):
        def chunk_rows(offset):
            return jnp.concatenate(
                [jnp.broadcast_to(a[c * CHUNK + offset:c * CHUNK + offset + 1, :], (CHUNK, PAIR_DIM))
                 for c in range(n_chunks)], axis=0)
        a_mid = chunk_rows(CHUNK // 2 - 1)
        a_end = chunk_rows(CHUNK - 1)
        q = _silu(proj_ref[hp % 2, Q])
        q_in = q * jnp.exp(a - a_mid)
        k_in = k * jnp.exp(a_mid - a)
        q_out = (q_in * jnp.exp(a_mid)).astype(_BF16)
        k_out = (k_in * jnp.exp(a_end - a_mid)).astype(_BF16)
        decay = [jnp.exp(a[c * CHUNK + CHUNK - 1:c * CHUNK + CHUNK, :]) for c in range(n_chunks)]
        k_in_t = k_in.T.astype(_BF16)
        return q_in.astype(_BF16), k_in_t, q_out, k_out, decay

    def scores_and_updates(hp, q_in, k_in_t, k_out):
        vv32 = proj_ref[hp % 2, I]
        vv = vv32.astype(_BF16)
        vv_t = vv32.T.astype(_BF16)
        scores = [_dot(q_in[:, ln], k_in_t[ln, :]) for ln in heads]
        kv = []
        for ln in heads:
            blocks = []
            for c, rw in enumerate(chunks):
                pieces = [k_out[rw, ln]]
                if c > 0:
                    pieces.insert(0, jnp.zeros((c * CHUNK, HG_HEAD_DIM), _BF16))
                if c + 1 < n_chunks:
                    pieces.append(jnp.zeros(((n_chunks - 1 - c) * CHUNK, HG_HEAD_DIM), _BF16))
                blocks.append(jnp.concatenate(pieces, axis=0))
            kv.append(_dot(vv_t[ln, :], jnp.concatenate(blocks, axis=1)))
        return vv, scores, kv

    def mask_and_states(hp, scores, kv, decay):
        scores = [jnp.where(intra, s, 0.0).astype(_BF16) for s in scores]
        states = []
        for hd, ln in enumerate(heads):
            st = state_ref[hp * 2 + hd]
            per_chunk = []
            for c in range(n_chunks):
                per_chunk.append(st.T.astype(_BF16))
                st = decay[c][:, ln] * st + kv[hd][:, c * HG_HEAD_DIM:(c + 1) * HG_HEAD_DIM]
            state_ref[hp * 2 + hd] = st
            states.append(per_chunk)
        return scores, states

    def outputs(vv, scores, q_out, states):
        o_heads = []
        for hd, ln in enumerate(heads):
            o_chunks = []
            for c, rw in enumerate(chunks):
                blk = slice((c // 2) * SG_BLOCK, (c // 2 + 1) * SG_BLOCK)
                lhs = jnp.concatenate([q_out[rw, ln], scores[hd][rw, blk]], axis=1)
                rhs = jnp.concatenate([states[hd][c], vv[blk, ln]], axis=0)
                o_chunks.append(_dot(lhs, rhs))
            o_heads.append(jnp.concatenate(o_chunks, axis=0))
        return o_heads

    def gated_norm(hp, o_heads):
        og = _silu(proj_ref[hp % 2, G])
        normed = [o * lax.rsqrt(jnp.mean(o * o, axis=-1, keepdims=True) + EPS) * gn for o in o_heads]
        return (jnp.concatenate(normed, axis=1) * og).astype(_BF16)

    def out_project(hp, y):
        for n in range(D_MODEL // PAIR_DIM):
            cols = slice(n * PAIR_DIM, (n + 1) * PAIR_DIM)
            contrib = _dot(y, wout_ref[hp, n])
            if hp == 0:
                acc_ref[:, cols] = contrib
            else:
                acc_ref[:, cols] += contrib

    for part in (F, Q, I, G):
        project(0, part)
    y_prev = None
    for hp in range(HEAD_PAIRS):
        nxt = hp + 1 if hp + 1 < HEAD_PAIRS else None
        k, terms = forget_gate(hp)
        a = prefix_sum(terms)
        if y_prev is not None:
            out_project(hp - 1, y_prev)
        if nxt is not None:
            project(nxt, F)
        q_in, k_in_t, q_out, k_out, decay = decay_scaling(hp, k, a)
        vv, scores, kv = scores_and_updates(hp, q_in, k_in_t, k_out)
        if nxt is not None:
            project(nxt, Q)
        scores, states = mask_and_states(hp, scores, kv, decay)
        o_heads = outputs(vv, scores, q_out, states)
        if nxt is not None:
            project(nxt, I)
            project(nxt, G)
        y_prev = gated_norm(hp, o_heads)
    out_project(HEAD_PAIRS - 1, y_prev)

    xo = x + mod_ref[0, 2:3, :] * acc_ref[...]
    r = lax.rsqrt(jnp.mean(xo * xo, axis=-1, keepdims=True) + EPS)
    o_ref[0] = xo * r * fgain_ref[...]


def _hgrn2_layer(x, mod, gain, w, lb_raw, gn_gain, w_out, final_gain, layer):
    bsz, seq, d = x.shape
    tm = TOKENS_PER_STEP
    const = lambda nd: (lambda b, t: (0,) * nd)
    resident = functools.partial(pl.BlockSpec, pipeline_mode=pl.Buffered(1))
    return pl.pallas_call(
        functools.partial(_hgrn2_kernel, layer=layer),
        grid=(bsz, seq // tm),
        in_specs=[
            pl.BlockSpec((1, tm, d), lambda b, t: (b, t, 0)),
            pl.BlockSpec((1, 3, d), lambda b, t: (b, 0, 0)),
            resident((1, d), const(2)),
            resident(w.shape, const(4)),
            resident(lb_raw.shape, const(3)),
            resident((1, HG_HEAD_DIM), const(2)),
            resident(w_out.shape, const(4)),
            resident((1, d), const(2)),
        ],
        out_specs=pl.BlockSpec((1, tm, d), lambda b, t: (b, t, 0)),
        out_shape=jax.ShapeDtypeStruct(x.shape, x.dtype),
        scratch_shapes=[
            pltpu.VMEM((tm, d), _BF16),
            pltpu.VMEM((tm, d), _F32),
            pltpu.VMEM((HG_HEADS, HG_HEAD_DIM, HG_HEAD_DIM), _F32),
            pltpu.VMEM((2, 4, tm, PAIR_DIM), _F32),
        ],
        compiler_params=pltpu.CompilerParams(
            dimension_semantics=("arbitrary", "arbitrary"), vmem_limit_bytes=VMEM_LIMIT_BYTES),
        name="hgrn2_layer",
    )(x, mod, gain, w, lb_raw, gn_gain, w_out, final_gain)


def kernel(x, c, norm_gain, w_ada, b_ada, a_w_in, a_ln_gain, a_ln_bias, a_w_s, a_b_s, a_w_out,
           b_w_in, b_lower_bounds, b_gn_gain, b_w_out, final_gain):
    bsz, seq, d = x.shape
    depth = norm_gain.shape[0]
    assert depth == 2 and a_w_in.shape[0] == 1 and b_w_in.shape[0] == 1
    assert seq % TOKENS_PER_STEP == 0 and TOKENS_PER_STEP % SG_BLOCK == 0

    mod = _adaln_mod(c, w_ada, b_ada)
    mod = mod.reshape(depth, bsz, 3, d)

    wa = a_w_in[0].astype(_BF16)
    wu = wa[:, :D_INNER].reshape(d, SG_GROUPS, SG_GROUP_DIM)
    wv = wa[:, D_INNER:2 * D_INNER]
    wg = wa[:, 2 * D_INNER:].reshape(d, SG_GROUPS, SG_GROUP_DIM)
    wug = jnp.concatenate([wu, wg], axis=-1).transpose(1, 0, 2)
    wout_a = a_w_out[0].astype(_BF16).reshape(SG_GROUPS, SG_GROUP_DIM, d)
    x1 = _sgu_layer(
        x, mod[0], norm_gain[0:1], wv, wug, a_ln_gain, a_ln_bias, a_w_s[0],
        a_b_s[0].reshape(SG_GROUPS, SG_BLOCK, 1), wout_a)

    wb = b_w_in[0].astype(_BF16).reshape(d, 4, HEAD_PAIRS, PAIR_DIM)
    wb = wb.transpose(2, 1, 0, 3)
    lb_raw = b_lower_bounds.reshape(depth, HEAD_PAIRS, PAIR_DIM).transpose(1, 0, 2)
    wout_b = b_w_out[0].astype(_BF16).reshape(HEAD_PAIRS, PAIR_DIM, d // PAIR_DIM, PAIR_DIM)
    wout_b = wout_b.transpose(0, 2, 1, 3)
    return _hgrn2_layer(
        x1, mod[1], norm_gain[1:2], wb, lb_raw, b_gn_gain, wout_b, final_gain.reshape(1, d),
        layer=1)
```

```python
import functools
import math

import jax
import jax.numpy as jnp
from jax import lax
from jax.experimental import pallas as pl
from jax.experimental.pallas import tpu as pltpu

D_MODEL = 1024
D_INNER = 2048
CHUNK = 64
SG_BLOCK = 128
SG_GROUPS = 8
SG_GROUP_DIM = D_INNER // SG_GROUPS
HG_HEADS = 16
HG_HEAD_DIM = D_INNER // HG_HEADS
HEAD_PAIRS = HG_HEADS // 2
PAIR_DIM = 2 * HG_HEAD_DIM
EPS = 1e-6

TOKENS_PER_STEP = 512
TILE_ROWS = 256
HGRN2_TILES_PER_STEP = 2
VMEM_LIMIT_BYTES = 56 * 1024 * 1024

_BF16 = jnp.bfloat16
_F32 = jnp.float32


def _dot(a, b):
    return jnp.dot(a, b, preferred_element_type=_F32)


def _sigmoid(z):
    return 1.0 / (1.0 + jnp.exp(-z))


def _silu(z):
    return z * _sigmoid(z)


def _gelu_tanh(z):
    c = math.sqrt(2.0 / math.pi)
    return 0.5 * z * (1.0 + jnp.tanh(c * (z + 0.044715 * (z * z * z))))


def _modulated_norm(x, gain, mod_ref):
    r = lax.rsqrt(jnp.mean(x * x, axis=-1, keepdims=True) + EPS)
    shift = mod_ref[0, 0:1, :]
    scale = mod_ref[0, 1:2, :]
    return (x * r * gain) * (1.0 + scale) + shift


def _adaln_kernel(c_ref, w_ref, b_ref, o_ref):
    c_act = _silu(c_ref[...]).astype(_BF16)
    o_ref[0] = _dot(c_act, w_ref[0].astype(_BF16)) + b_ref[0]


def _adaln_mod(c, w_ada, b_ada):
    depth, d, n3 = w_ada.shape
    bsz = c.shape[0]
    nt = n3 // d
    return pl.pallas_call(
        _adaln_kernel,
        grid=(depth, nt),
        in_specs=[
            pl.BlockSpec((bsz, d), lambda l, j: (0, 0)),
            pl.BlockSpec((1, d, d), lambda l, j: (l, 0, j)),
            pl.BlockSpec((1, 1, d), lambda l, j: (l, 0, j)),
        ],
        out_specs=pl.BlockSpec((1, bsz, d), lambda l, j: (l, 0, j)),
        out_shape=jax.ShapeDtypeStruct((depth, bsz, n3), _F32),
        compiler_params=pltpu.CompilerParams(dimension_semantics=("arbitrary", "arbitrary")),
        name="adaln_mod",
    )(c, w_ada, b_ada.reshape(depth, 1, n3))


def _sgu_kernel(x_ref, mod_ref, gain_ref, wv_ref, wug_ref, lng_ref, lnb_ref, ws_ref, bs_ref,
                wout_ref, o_ref, h_ref, v_ref, ug_ref, acc_ref):
    tm = x_ref.shape[1]
    x = x_ref[0]
    h_ref[...] = _modulated_norm(x, gain_ref[...], mod_ref).astype(_BF16)

    t_chunk = lax.broadcasted_iota(jnp.int32, (SG_BLOCK, SG_BLOCK), 0) // CHUNK
    s_chunk = lax.broadcasted_iota(jnp.int32, (SG_BLOCK, SG_BLOCK), 1) // CHUNK
    allowed = s_chunk <= t_chunk

    def group_cols(g):
        return slice(g * SG_GROUP_DIM, (g + 1) * SG_GROUP_DIM)

    def project_ug(g):
        ug_ref[g % 2] = _dot(h_ref[...], wug_ref[g])

    def lane_halves_sum(z):
        return z[:, :SG_GROUP_DIM // 2] + z[:, SG_GROUP_DIM // 2:]

    row_sum = None
    for g in range(SG_GROUPS):
        vg = _gelu_tanh(_dot(h_ref[...], wv_ref[g]))
        v_ref[g] = vg
        row_sum = lane_halves_sum(vg) if row_sum is None else row_sum + lane_halves_sum(vg)
    mu = jnp.sum(row_sum, axis=-1, keepdims=True) * (1.0 / D_INNER)
    project_ug(0)
    sq_sum = None
    for g in range(SG_GROUPS):
        vc = v_ref[g] - mu
        sq_sum = lane_halves_sum(vc * vc) if sq_sum is None else sq_sum + lane_halves_sum(vc * vc)
    var = jnp.sum(sq_sum, axis=-1, keepdims=True) * (1.0 / D_INNER)
    rstd = lax.rsqrt(var + EPS)

    for g in range(SG_GROUPS):
        vn = ((v_ref[g] - mu) * rstd * lng_ref[:, group_cols(g)] + lnb_ref[:, group_cols(g)]).astype(_BF16)
        ws = jnp.where(allowed, ws_ref[g], 0.0).astype(_BF16)
        bias = bs_ref[g]
        s = jnp.concatenate(
            [_dot(ws, vn[blk * SG_BLOCK:(blk + 1) * SG_BLOCK, :]) + bias for blk in range(tm // SG_BLOCK)],
            axis=0)
        if g + 1 < SG_GROUPS:
            project_ug(g + 1)
        u = _gelu_tanh(ug_ref[g % 2, :, :SG_GROUP_DIM])
        gate = _silu(ug_ref[g % 2, :, SG_GROUP_DIM:])
        y = (u * s * gate).astype(_BF16)
        for n in range(D_MODEL // SG_GROUP_DIM):
            cols = slice(n * SG_GROUP_DIM, (n + 1) * SG_GROUP_DIM)
            contrib = _dot(y, wout_ref[g, n])
            if g == 0:
                acc_ref[:, cols] = contrib
            else:
                acc_ref[:, cols] += contrib

    o_ref[0] = x + mod_ref[0, 2:3, :] * acc_ref[...]


def _sgu_layer(x, mod, gain, wv, wug, ln_gain, ln_bias, w_s, b_s, w_out):
    bsz, seq, d = x.shape
    tm = TOKENS_PER_STEP
    const = lambda nd: (lambda b, t: (0,) * nd)
    resident = functools.partial(pl.BlockSpec, pipeline_mode=pl.Buffered(1))
    return pl.pallas_call(
        _sgu_kernel,
        grid=(bsz, seq // tm),
        in_specs=[
            pl.BlockSpec((1, tm, d), lambda b, t: (b, t, 0)),
            pl.BlockSpec((1, 3, d), lambda b, t: (b, 0, 0)),
            resident((1, d), const(2)),
            resident(wv.shape, const(3)),
            resident(wug.shape, const(3)),
            resident((1, D_INNER), const(2)),
            resident((1, D_INNER), const(2)),
            resident(w_s.shape, const(3)),
            resident(b_s.shape, const(3)),
            resident(w_out.shape, const(4)),
        ],
        out_specs=pl.BlockSpec((1, tm, d), lambda b, t: (b, t, 0)),
        out_shape=jax.ShapeDtypeStruct(x.shape, x.dtype),
        scratch_shapes=[
            pltpu.VMEM((tm, d), _BF16),
            pltpu.VMEM((SG_GROUPS, tm, SG_GROUP_DIM), _F32),
            pltpu.VMEM((2, tm, 2 * SG_GROUP_DIM), _F32),
            pltpu.VMEM((tm, d), _F32),
        ],
        compiler_params=pltpu.CompilerParams(
            dimension_semantics=("arbitrary", "arbitrary"), vmem_limit_bytes=VMEM_LIMIT_BYTES),
        name="sgu_layer",
    )(x, mod, gain, wv, wug, ln_gain, ln_bias, w_s, b_s, w_out)


def _hgrn2_kernel(x_ref, mod_ref, gain_ref, w_ref, lbraw_ref, gn_ref, wout_ref, fgain_ref,
                  o_ref, h_ref, acc_ref, state_ref, proj_ref, *, layer):
    tm = TILE_ROWS
    n_tiles = x_ref.shape[1] // tm
    n_items = n_tiles * HEAD_PAIRS
    n_chunks = tm // CHUNK
    Q, F, I, G = range(4)

    @pl.when(pl.program_id(1) == 0)
    def _():
        state_ref[...] = jnp.zeros_like(state_ref)

    row = lax.broadcasted_iota(jnp.int32, (tm, tm), 0)
    col = lax.broadcasted_iota(jnp.int32, (tm, tm), 1)
    intra = (row // CHUNK == col // CHUNK) & (col <= row)
    step_in_chunk = lax.broadcasted_iota(jnp.int32, (tm, PAIR_DIM), 0) % CHUNK
    gn = gn_ref[...]
    heads = [slice(hd * HG_HEAD_DIM, (hd + 1) * HG_HEAD_DIM) for hd in range(2)]
    chunks = [slice(c * CHUNK, (c + 1) * CHUNK) for c in range(n_chunks)]
    row_halves = [slice(0, tm // 2), slice(tm // 2, tm)]

    def tile_rows(tile):
        return slice(tile * tm, (tile + 1) * tm)

    def normalize_tile(tile):
        h_ref[tile] = _modulated_norm(x_ref[0, tile_rows(tile)], gain_ref[...], mod_ref).astype(_BF16)

    def finish_tile(tile):
        xo = x_ref[0, tile_rows(tile)] + mod_ref[0, 2:3, :] * acc_ref[tile]
        r = lax.rsqrt(jnp.mean(xo * xo, axis=-1, keepdims=True) + EPS)
        o_ref[0, tile_rows(tile)] = xo * r * fgain_ref[...]

    def project(it, part):
        tile, hp = divmod(it, HEAD_PAIRS)
        for rows in row_halves:
            proj_ref[it % 2, part, rows] = _dot(h_ref[tile, rows], w_ref[hp, part])

    def forget_gate(it):
        hp = it % HEAD_PAIRS
        lbraw = lbraw_ref[hp]
        e = jnp.exp(lbraw - jnp.max(lbraw, axis=0, keepdims=True))
        p = e / jnp.sum(e, axis=0, keepdims=True)
        lb = jnp.sum(p[1:layer + 1], axis=0, keepdims=True)
        f = lb + (1.0 - lb) * _sigmoid(proj_ref[it % 2, F])
        return 1.0 - f, jnp.log(f)

    def prefix_sum(z):
        shift = 1
        while shift < CHUNK:
            z = z + jnp.where(step_in_chunk >= shift, pltpu.roll(z, shift, axis=0), 0.0)
            shift *= 2
        return z

    def decay_scaling(it, k, a):
        def chunk_rows(offset):
            return jnp.concatenate(
                [jnp.broadcast_to(a[c * CHUNK + offset:c * CHUNK + offset + 1, :], (CHUNK, PAIR_DIM))
                 for c in range(n_chunks)], axis=0)
        a_mid = chunk_rows(CHUNK // 2 - 1)
        a_end = chunk_rows(CHUNK - 1)
        q = _silu(proj_ref[it % 2, Q])
        q_in = q * jnp.exp(a - a_mid)
        k_in = k * jnp.exp(a_mid - a)
        q_out = (q_in * jnp.exp(a_mid)).astype(_BF16)
        k_out = (k_in * jnp.exp(a_end - a_mid)).astype(_BF16)
        decay = [jnp.exp(a[c * CHUNK + CHUNK - 1:c * CHUNK + CHUNK, :]) for c in range(n_chunks)]
        k_in_t = k_in.T.astype(_BF16)
        return q_in.astype(_BF16), k_in_t, q_out, k_out, decay

    def scores_and_updates(it, q_in, k_in_t, k_out):
        vv32 = proj_ref[it % 2, I]
        vv = vv32.astype(_BF16)
        vv_t = vv32.T.astype(_BF16)
        scores = [_dot(q_in[:, ln], k_in_t[ln, :]) for ln in heads]
        kv = []
        for ln in heads:
            blocks = []
            for c, rw in enumerate(chunks):
                pieces = [k_out[rw, ln]]
                if c > 0:
                    pieces.insert(0, jnp.zeros((c * CHUNK, HG_HEAD_DIM), _BF16))
                if c + 1 < n_chunks:
                    pieces.append(jnp.zeros(((n_chunks - 1 - c) * CHUNK, HG_HEAD_DIM), _BF16))
                blocks.append(jnp.concatenate(pieces, axis=0))
            kv.append(_dot(vv_t[ln, :], jnp.concatenate(blocks, axis=1)))
        return vv, scores, kv

    def mask_and_states(it, scores, kv, decay):
        hp = it % HEAD_PAIRS
        scores = [jnp.where(intra, s, 0.0).astype(_BF16) for s in scores]
        states = []
        for hd, ln in enumerate(heads):
            st = state_ref[hp * 2 + hd]
            per_chunk = []
            for c in range(n_chunks):
                per_chunk.append(st.T.astype(_BF16))
                st = decay[c][:, ln] * st + kv[hd][:, c * HG_HEAD_DIM:(c + 1) * HG_HEAD_DIM]
            state_ref[hp * 2 + hd] = st
            states.append(per_chunk)
        return scores, states

    def outputs(vv, scores, q_out, states):
        o_heads = []
        for hd, ln in enumerate(heads):
            o_chunks = []
            for c, rw in enumerate(chunks):
                blk = slice((c // 2) * SG_BLOCK, (c // 2 + 1) * SG_BLOCK)
                lhs = jnp.concatenate([q_out[rw, ln], scores[hd][rw, blk]], axis=1)
                rhs = jnp.concatenate([states[hd][c], vv[blk, ln]], axis=0)
                o_chunks.append(_dot(lhs, rhs))
            o_heads.append(jnp.concatenate(o_chunks, axis=0))
        return o_heads

    def gated_norm(it, o_heads):
        og = _silu(proj_ref[it % 2, G])
        normed = [o * lax.rsqrt(jnp.mean(o * o, axis=-1, keepdims=True) + EPS) * gn for o in o_heads]
        return (jnp.concatenate(normed, axis=1) * og).astype(_BF16)

    def out_project(it, y):
        tile, hp = divmod(it, HEAD_PAIRS)
        for n in range(D_MODEL // PAIR_DIM):
            cols = slice(n * PAIR_DIM, (n + 1) * PAIR_DIM)
            contrib = _dot(y, wout_ref[hp, n])
            if hp == 0:
                acc_ref[tile, :, cols] = contrib
            else:
                acc_ref[tile, :, cols] += contrib
        if hp == HEAD_PAIRS - 1:
            finish_tile(tile)

    normalize_tile(0)
    for part in (F, Q, I, G):
        project(0, part)
    y_prev = None
    for it in range(n_items):
        tile, hp = divmod(it, HEAD_PAIRS)
        nxt = it + 1 if it + 1 < n_items else None
        if hp == HEAD_PAIRS // 2 and tile + 1 < n_tiles:
            normalize_tile(tile + 1)
        k, log_f = forget_gate(it)
        a = prefix_sum(log_f)
        if y_prev is not None:
            out_project(it - 1, y_prev)
        if nxt is not None:
            project(nxt, F)
        q_in, k_in_t, q_out, k_out, decay = decay_scaling(it, k, a)
        vv, scores, kv = scores_and_updates(it, q_in, k_in_t, k_out)
        if nxt is not None:
            project(nxt, Q)
        scores, states = mask_and_states(it, scores, kv, decay)
        o_heads = outputs(vv, scores, q_out, states)
        if nxt is not None:
            project(nxt, I)
            project(nxt, G)
        y_prev = gated_norm(it, o_heads)
    out_project(n_items - 1, y_prev)


def _hgrn2_layer(x, mod, gain, w, lb_raw, gn_gain, w_out, final_gain, layer):
    bsz, seq, d = x.shape
    tm = HGRN2_TILES_PER_STEP * TILE_ROWS
    const = lambda nd: (lambda b, t: (0,) * nd)
    resident = functools.partial(pl.BlockSpec, pipeline_mode=pl.Buffered(1))
    return pl.pallas_call(
        functools.partial(_hgrn2_kernel, layer=layer),
        grid=(bsz, seq // tm),
        in_specs=[
            pl.BlockSpec((1, tm, d), lambda b, t: (b, t, 0)),
            pl.BlockSpec((1, 3, d), lambda b, t: (b, 0, 0)),
            resident((1, d), const(2)),
            resident(w.shape, const(4)),
            resident(lb_raw.shape, const(3)),
            resident((1, HG_HEAD_DIM), const(2)),
            resident(w_out.shape, const(4)),
            resident((1, d), const(2)),
        ],
        out_specs=pl.BlockSpec((1, tm, d), lambda b, t: (b, t, 0)),
        out_shape=jax.ShapeDtypeStruct(x.shape, x.dtype),
        scratch_shapes=[
            pltpu.VMEM((HGRN2_TILES_PER_STEP, TILE_ROWS, d), _BF16),
            pltpu.VMEM((HGRN2_TILES_PER_STEP, TILE_ROWS, d), _F32),
            pltpu.VMEM((HG_HEADS, HG_HEAD_DIM, HG_HEAD_DIM), _F32),
            pltpu.VMEM((2, 4, TILE_ROWS, PAIR_DIM), _F32),
        ],
        compiler_params=pltpu.CompilerParams(
            dimension_semantics=("arbitrary", "arbitrary"), vmem_limit_bytes=VMEM_LIMIT_BYTES),
        name="hgrn2_layer",
    )(x, mod, gain, w, lb_raw, gn_gain, w_out, final_gain)


def kernel(x, c, norm_gain, w_ada, b_ada, a_w_in, a_ln_gain, a_ln_bias, a_w_s, a_b_s, a_w_out,
           b_w_in, b_lower_bounds, b_gn_gain, b_w_out, final_gain):
    bsz, seq, d = x.shape
    depth = norm_gain.shape[0]
    assert depth == 2 and a_w_in.shape[0] == 1 and b_w_in.shape[0] == 1
    assert seq % TOKENS_PER_STEP == 0 and TOKENS_PER_STEP % SG_BLOCK == 0
    assert seq % (HGRN2_TILES_PER_STEP * TILE_ROWS) == 0

    mod = _adaln_mod(c, w_ada, b_ada)
    mod = mod.reshape(depth, bsz, 3, d)

    wa = a_w_in[0].astype(_BF16)
    wu = wa[:, :D_INNER].reshape(d, SG_GROUPS, SG_GROUP_DIM)
    wv = wa[:, D_INNER:2 * D_INNER].reshape(d, SG_GROUPS, SG_GROUP_DIM).transpose(1, 0, 2)
    wg = wa[:, 2 * D_INNER:].reshape(d, SG_GROUPS, SG_GROUP_DIM)
    wug = jnp.concatenate([wu, wg], axis=-1).transpose(1, 0, 2)
    wout_a = a_w_out[0].astype(_BF16).reshape(SG_GROUPS, SG_GROUP_DIM, d // SG_GROUP_DIM, SG_GROUP_DIM)
    wout_a = wout_a.transpose(0, 2, 1, 3)
    x1 = _sgu_layer(
        x, mod[0], norm_gain[0:1], wv, wug, a_ln_gain, a_ln_bias, a_w_s[0],
        a_b_s[0].reshape(SG_GROUPS, SG_BLOCK, 1), wout_a)

    wb = b_w_in[0].astype(_BF16).reshape(d, 4, HEAD_PAIRS, PAIR_DIM)
    wb = wb.transpose(2, 1, 0, 3)
    lb_raw = b_lower_bounds.reshape(depth, HEAD_PAIRS, PAIR_DIM).transpose(1, 0, 2)
    wout_b = b_w_out[0].astype(_BF16).reshape(HEAD_PAIRS, PAIR_DIM, d // PAIR_DIM, PAIR_DIM)
    wout_b = wout_b.transpose(0, 2, 1, 3)
    return _hgrn2_layer(
        x1, mod[1], norm_gain[1:2], wb, lb_raw, b_gn_gain, wout_b, final_gain.reshape(1, d),
        layer=1)
```

```python
import functools
import math

import jax
import jax.numpy as jnp
from jax import lax
from jax.experimental import pallas as pl
from jax.experimental.pallas import tpu as pltpu

D_MODEL = 1024
D_INNER = 2048
CHUNK = 64
SG_BLOCK = 128
SG_GROUPS = 8
SG_GROUP_DIM = D_INNER // SG_GROUPS
HG_HEADS = 16
HG_HEAD_DIM = D_INNER // HG_HEADS
HEAD_PAIRS = HG_HEADS // 2
PAIR_DIM = 2 * HG_HEAD_DIM
EPS = 1e-6

TOKENS_PER_STEP = 256
TILE_ROWS = 256
HGRN2_TILES_PER_STEP = 1
VMEM_LIMIT_BYTES = 56 * 1024 * 1024

_BF16 = jnp.bfloat16
_F32 = jnp.float32


def _dot(a, b):
    return jnp.dot(a, b, preferred_element_type=_F32)


def _sigmoid(z):
    return 1.0 / (1.0 + jnp.exp(-z))


def _silu(z):
    return z * _sigmoid(z)


def _gelu_tanh(z):
    c = math.sqrt(2.0 / math.pi)
    return 0.5 * z * (1.0 + jnp.tanh(c * (z + 0.044715 * (z * z * z))))


def _modulated_norm(x, gain, mod_ref):
    r = lax.rsqrt(jnp.mean(x * x, axis=-1, keepdims=True) + EPS)
    shift = mod_ref[0, 0:1, :]
    scale = mod_ref[0, 1:2, :]
    return (x * r * gain) * (1.0 + scale) + shift


def _adaln_kernel(c_ref, w_ref, b_ref, o_ref):
    c_act = _silu(c_ref[...]).astype(_BF16)
    o_ref[0] = _dot(c_act, w_ref[0].astype(_BF16)) + b_ref[0]


def _adaln_mod(c, w_ada, b_ada):
    depth, d, n3 = w_ada.shape
    bsz = c.shape[0]
    nt = n3 // d
    return pl.pallas_call(
        _adaln_kernel,
        grid=(depth, nt),
        in_specs=[
            pl.BlockSpec((bsz, d), lambda l, j: (0, 0)),
            pl.BlockSpec((1, d, d), lambda l, j: (l, 0, j)),
            pl.BlockSpec((1, 1, d), lambda l, j: (l, 0, j)),
        ],
        out_specs=pl.BlockSpec((1, bsz, d), lambda l, j: (l, 0, j)),
        out_shape=jax.ShapeDtypeStruct((depth, bsz, n3), _F32),
        compiler_params=pltpu.CompilerParams(dimension_semantics=("arbitrary", "arbitrary")),
        name="adaln_mod",
    )(c, w_ada, b_ada.reshape(depth, 1, n3))


def _sgu_kernel(x_ref, mod_ref, gain_ref, win_ref, lng_ref, lnb_ref, ws_ref, bs_ref,
                wout_ref, o_ref, h_ref, v_ref, ug_ref, acc_ref):
    tm = x_ref.shape[1]
    U, V, GATE = range(3)
    x = x_ref[0]
    h_ref[...] = _modulated_norm(x, gain_ref[...], mod_ref).astype(_BF16)

    t_chunk = lax.broadcasted_iota(jnp.int32, (SG_BLOCK, SG_BLOCK), 0) // CHUNK
    s_chunk = lax.broadcasted_iota(jnp.int32, (SG_BLOCK, SG_BLOCK), 1) // CHUNK
    allowed = s_chunk <= t_chunk

    def group_cols(g):
        return slice(g * SG_GROUP_DIM, (g + 1) * SG_GROUP_DIM)

    def project_ug(g):
        ug_ref[g % 2, 0] = _dot(h_ref[...], win_ref[U, g])
        ug_ref[g % 2, 1] = _dot(h_ref[...], win_ref[GATE, g])

    def lane_halves_sum(z):
        return z[:, :SG_GROUP_DIM // 2] + z[:, SG_GROUP_DIM // 2:]

    row_sum = None
    for g in range(SG_GROUPS):
        vg = _gelu_tanh(_dot(h_ref[...], win_ref[V, g]))
        v_ref[g] = vg
        row_sum = lane_halves_sum(vg) if row_sum is None else row_sum + lane_halves_sum(vg)
    mu = jnp.sum(row_sum, axis=-1, keepdims=True) * (1.0 / D_INNER)
    project_ug(0)
    sq_sum = None
    for g in range(SG_GROUPS):
        vc = v_ref[g] - mu
        sq_sum = lane_halves_sum(vc * vc) if sq_sum is None else sq_sum + lane_halves_sum(vc * vc)
    var = jnp.sum(sq_sum, axis=-1, keepdims=True) * (1.0 / D_INNER)
    rstd = lax.rsqrt(var + EPS)

    for g in range(SG_GROUPS):
        vn = ((v_ref[g] - mu) * rstd * lng_ref[:, group_cols(g)] + lnb_ref[:, group_cols(g)]).astype(_BF16)
        ws = jnp.where(allowed, ws_ref[g], 0.0).astype(_BF16)
        bias = bs_ref[g]
        s = jnp.concatenate(
            [_dot(ws, vn[blk * SG_BLOCK:(blk + 1) * SG_BLOCK, :]) + bias for blk in range(tm // SG_BLOCK)],
            axis=0)
        if g + 1 < SG_GROUPS:
            project_ug(g + 1)
        u = _gelu_tanh(ug_ref[g % 2, 0])
        gate = _silu(ug_ref[g % 2, 1])
        y = (u * s * gate).astype(_BF16)
        contrib = _dot(y, wout_ref[g])
        if g == 0:
            acc_ref[...] = contrib
        else:
            acc_ref[...] += contrib

    o_ref[0] = x + mod_ref[0, 2:3, :] * acc_ref[...]


def _sgu_layer(x, mod, gain, w_in, ln_gain, ln_bias, w_s, b_s, w_out):
    bsz, seq, d = x.shape
    tm = TOKENS_PER_STEP
    const = lambda nd: (lambda b, t: (0,) * nd)
    resident = functools.partial(pl.BlockSpec, pipeline_mode=pl.Buffered(1))
    return pl.pallas_call(
        _sgu_kernel,
        grid=(bsz, seq // tm),
        in_specs=[
            pl.BlockSpec((1, tm, d), lambda b, t: (b, t, 0)),
            pl.BlockSpec((1, 3, d), lambda b, t: (b, 0, 0)),
            resident((1, d), const(2)),
            resident(w_in.shape, const(4)),
            resident((1, D_INNER), const(2)),
            resident((1, D_INNER), const(2)),
            resident(w_s.shape, const(3)),
            resident(b_s.shape, const(3)),
            resident(w_out.shape, const(3)),
        ],
        out_specs=pl.BlockSpec((1, tm, d), lambda b, t: (b, t, 0)),
        out_shape=jax.ShapeDtypeStruct(x.shape, x.dtype),
        scratch_shapes=[
            pltpu.VMEM((tm, d), _BF16),
            pltpu.VMEM((SG_GROUPS, tm, SG_GROUP_DIM), _F32),
            pltpu.VMEM((2, 2, tm, SG_GROUP_DIM), _F32),
            pltpu.VMEM((tm, d), _F32),
        ],
        compiler_params=pltpu.CompilerParams(
            dimension_semantics=("arbitrary", "arbitrary"), vmem_limit_bytes=VMEM_LIMIT_BYTES),
        name="sgu_layer",
    )(x, mod, gain, w_in, ln_gain, ln_bias, w_s, b_s, w_out)


def _hgrn2_kernel(x_ref, mod_ref, gain_ref, w_ref, lbraw_ref, gn_ref, wout_ref, fgain_ref,
                  o_ref, h_ref, acc_ref, state_ref, proj_ref, *, layer):
    tm = TILE_ROWS
    n_tiles = x_ref.shape[1] // tm
    n_items = n_tiles * HEAD_PAIRS
    n_chunks = tm // CHUNK
    Q, F, I, G = range(4)

    @pl.when(pl.program_id(1) == 0)
    def _():
        state_ref[...] = jnp.zeros_like(state_ref)

    row = lax.broadcasted_iota(jnp.int32, (tm, tm), 0)
    col = lax.broadcasted_iota(jnp.int32, (tm, tm), 1)
    intra = (row // CHUNK == col // CHUNK) & (col <= row)
    step_in_chunk = lax.broadcasted_iota(jnp.int32, (tm, PAIR_DIM), 0) % CHUNK
    gn = gn_ref[...]
    heads = [slice(hd * HG_HEAD_DIM, (hd + 1) * HG_HEAD_DIM) for hd in range(2)]
    chunks = [slice(c * CHUNK, (c + 1) * CHUNK) for c in range(n_chunks)]
    row_halves = [slice(0, tm // 2), slice(tm // 2, tm)]

    def tile_rows(tile):
        return slice(tile * tm, (tile + 1) * tm)

    def normalize_tile(tile):
        h_ref[tile] = _modulated_norm(x_ref[0, tile_rows(tile)], gain_ref[...], mod_ref).astype(_BF16)

    def finish_tile(tile):
        xo = x_ref[0, tile_rows(tile)] + mod_ref[0, 2:3, :] * acc_ref[tile]
        r = lax.rsqrt(jnp.mean(xo * xo, axis=-1, keepdims=True) + EPS)
        o_ref[0, tile_rows(tile)] = xo * r * fgain_ref[...]

    def project(it, part):
        tile, hp = divmod(it, HEAD_PAIRS)
        for rows in row_halves:
            proj_ref[it % 2, part, rows] = _dot(h_ref[tile, rows], w_ref[hp, part])

    def forget_gate(it):
        hp = it % HEAD_PAIRS
        lbraw = lbraw_ref[hp]
        e = jnp.exp(lbraw - jnp.max(lbraw, axis=0, keepdims=True))
        p = e / jnp.sum(e, axis=0, keepdims=True)
        lb = jnp.sum(p[1:layer + 1], axis=0, keepdims=True)
        f = lb + (1.0 - lb) * _sigmoid(proj_ref[it % 2, F])
        return 1.0 - f, jnp.log(f)

    def prefix_sum(z):
        shift = 1
        while shift < CHUNK:
            z = z + jnp.where(step_in_chunk >= shift, pltpu.roll(z, shift, axis=0), 0.0)
            shift *= 2
        return z

    def decay_scaling(it, k, a):
        def chunk_rows(offset):
            return jnp.concatenate(
                [jnp.broadcast_to(a[c * CHUNK + offset:c * CHUNK + offset + 1, :], (CHUNK, PAIR_DIM))
                 for c in range(n_chunks)], axis=0)
        a_mid = chunk_rows(CHUNK // 2 - 1)
        a_end = chunk_rows(CHUNK - 1)
        q = _silu(proj_ref[it % 2, Q])
        q_in = q * jnp.exp(a - a_mid)
        k_in = k * jnp.exp(a_mid - a)
        q_out = (q_in * jnp.exp(a_mid)).astype(_BF16)
        k_out = (k_in * jnp.exp(a_end - a_mid)).astype(_BF16)
        decay = [jnp.exp(a[c * CHUNK + CHUNK - 1:c * CHUNK + CHUNK, :]) for c in range(n_chunks)]
        k_in_t = k_in.T.astype(_BF16)
        return q_in.astype(_BF16), k_in_t, q_out, k_out, decay

    def scores_and_updates(it, q_in, k_in_t, k_out):
        vv32 = proj_ref[it % 2, I]
        vv = vv32.astype(_BF16)
        vv_t = vv32.T.astype(_BF16)
        scores = [_dot(q_in[:, ln], k_in_t[ln, :]) for ln in heads]
        kv = []
        for ln in heads:
            blocks = []
            for c, rw in enumerate(chunks):
                pieces = [k_out[rw, ln]]
                if c > 0:
                    pieces.insert(0, jnp.zeros((c * CHUNK, HG_HEAD_DIM), _BF16))
                if c + 1 < n_chunks:
                    pieces.append(jnp.zeros(((n_chunks - 1 - c) * CHUNK, HG_HEAD_DIM), _BF16))
                blocks.append(jnp.concatenate(pieces, axis=0))
            kv.append(_dot(vv_t[ln, :], jnp.concatenate(blocks, axis=1)))
        return vv, scores, kv

    def mask_and_states(it, scores, kv, decay):
        hp = it % HEAD_PAIRS
        scores = [jnp.where(intra, s, 0.0).astype(_BF16) for s in scores]
        states = []
        for hd, ln in enumerate(heads):
            st = state_ref[hp * 2 + hd]
            per_chunk = []
            for c in range(n_chunks):
                per_chunk.append(st.T.astype(_BF16))
                st = decay[c][:, ln] * st + kv[hd][:, c * HG_HEAD_DIM:(c + 1) * HG_HEAD_DIM]
            state_ref[hp * 2 + hd] = st
            states.append(per_chunk)
        return scores, states

    def outputs(vv, scores, q_out, states):
        o_heads = []
        for hd, ln in enumerate(heads):
            o_chunks = []
            for c, rw in enumerate(chunks):
                blk = slice((c // 2) * SG_BLOCK, (c // 2 + 1) * SG_BLOCK)
                lhs = jnp.concatenate([q_out[rw, ln], scores[hd][rw, blk]], axis=1)
                rhs = jnp.concatenate([states[hd][c], vv[blk, ln]], axis=0)
                o_chunks.append(_dot(lhs, rhs))
            o_heads.append(jnp.concatenate(o_chunks, axis=0))
        return o_heads

    def gated_norm(it, o_heads):
        og = _silu(proj_ref[it % 2, G])
        normed = [o * lax.rsqrt(jnp.mean(o * o, axis=-1, keepdims=True) + EPS) * gn for o in o_heads]
        return (jnp.concatenate(normed, axis=1) * og).astype(_BF16)

    def out_project(it, y):
        tile, hp = divmod(it, HEAD_PAIRS)
        for n in range(D_MODEL // PAIR_DIM):
            cols = slice(n * PAIR_DIM, (n + 1) * PAIR_DIM)
            contrib = _dot(y, wout_ref[hp, n])
            if hp == 0:
                acc_ref[tile, :, cols] = contrib
            else:
                acc_ref[tile, :, cols] += contrib
        if hp == HEAD_PAIRS - 1:
            finish_tile(tile)

    normalize_tile(0)
    for part in (F, Q, I, G):
        project(0, part)
    y_prev = None
    for it in range(n_items):
        tile, hp = divmod(it, HEAD_PAIRS)
        nxt = it + 1 if it + 1 < n_items else None
        if hp == HEAD_PAIRS // 2 and tile + 1 < n_tiles:
            normalize_tile(tile + 1)
        k, log_f = forget_gate(it)
        a = prefix_sum(log_f)
        if y_prev is not None:
            out_project(it - 1, y_prev)
        if nxt is not None:
            project(nxt, F)
        q_in, k_in_t, q_out, k_out, decay = decay_scaling(it, k, a)
        vv, scores, kv = scores_and_updates(it, q_in, k_in_t, k_out)
        if nxt is not None:
            project(nxt, Q)
        scores, states = mask_and_states(it, scores, kv, decay)
        o_heads = outputs(vv, scores, q_out, states)
        if nxt is not None:
            project(nxt, I)
            project(nxt, G)
        y_prev = gated_norm(it, o_heads)
    out_project(n_items - 1, y_prev)


def _hgrn2_layer(x, mod, gain, w, lb_raw, gn_gain, w_out, final_gain, layer):
    bsz, seq, d = x.shape
    tm = HGRN2_TILES_PER_STEP * TILE_ROWS
    const = lambda nd: (lambda b, t: (0,) * nd)
    resident = functools.partial(pl.BlockSpec, pipeline_mode=pl.Buffered(1))
    return pl.pallas_call(
        functools.partial(_hgrn2_kernel, layer=layer),
        grid=(bsz, seq // tm),
        in_specs=[
            pl.BlockSpec((1, tm, d), lambda b, t: (b, t, 0)),
            pl.BlockSpec((1, 3, d), lambda b, t: (b, 0, 0)),
            resident((1, d), const(2)),
            resident(w.shape, const(4)),
            resident(lb_raw.shape, const(3)),
            resident((1, HG_HEAD_DIM), const(2)),
            resident(w_out.shape, const(4)),
            resident((1, d), const(2)),
        ],
        out_specs=pl.BlockSpec((1, tm, d), lambda b, t: (b, t, 0)),
        out_shape=jax.ShapeDtypeStruct(x.shape, x.dtype),
        scratch_shapes=[
            pltpu.VMEM((HGRN2_TILES_PER_STEP, TILE_ROWS, d), _BF16),
            pltpu.VMEM((HGRN2_TILES_PER_STEP, TILE_ROWS, d), _F32),
            pltpu.VMEM((HG_HEADS, HG_HEAD_DIM, HG_HEAD_DIM), _F32),
            pltpu.VMEM((2, 4, TILE_ROWS, PAIR_DIM), _F32),
        ],
        compiler_params=pltpu.CompilerParams(
            dimension_semantics=("arbitrary", "arbitrary"), vmem_limit_bytes=VMEM_LIMIT_BYTES),
        name="hgrn2_layer",
    )(x, mod, gain, w, lb_raw, gn_gain, w_out, final_gain)


def kernel(x, c, norm_gain, w_ada, b_ada, a_w_in, a_ln_gain, a_ln_bias, a_w_s, a_b_s, a_w_out,
           b_w_in, b_lower_bounds, b_gn_gain, b_w_out, final_gain):
    bsz, seq, d = x.shape
    depth = norm_gain.shape[0]
    assert depth == 2 and a_w_in.shape[0] == 1 and b_w_in.shape[0] == 1
    assert seq % TOKENS_PER_STEP == 0 and TOKENS_PER_STEP % SG_BLOCK == 0
    assert seq % (HGRN2_TILES_PER_STEP * TILE_ROWS) == 0

    mod = _adaln_mod(c, w_ada, b_ada)
    mod = mod.reshape(depth, bsz, 3, d)

    wa = a_w_in[0].astype(_BF16).reshape(d, 3, SG_GROUPS, SG_GROUP_DIM).transpose(1, 2, 0, 3)
    wout_a = a_w_out[0].astype(_BF16).reshape(SG_GROUPS, SG_GROUP_DIM, d)
    x1 = _sgu_layer(
        x, mod[0], norm_gain[0:1], wa, a_ln_gain, a_ln_bias, a_w_s[0],
        a_b_s[0].reshape(SG_GROUPS, SG_BLOCK, 1), wout_a)

    wb = b_w_in[0].astype(_BF16).reshape(d, 4, HEAD_PAIRS, PAIR_DIM)
    wb = wb.transpose(2, 1, 0, 3)
    lb_raw = b_lower_bounds.reshape(depth, HEAD_PAIRS, PAIR_DIM).transpose(1, 0, 2)
    wout_b = b_w_out[0].astype(_BF16).reshape(HEAD_PAIRS, PAIR_DIM, d // PAIR_DIM, PAIR_DIM)
    wout_b = wout_b.transpose(0, 2, 1, 3)
    return _hgrn2_layer(
        x1, mod[1], norm_gain[1:2], wb, lb_raw, b_gn_gain, wout_b, final_gain.reshape(1, d),
        layer=1)
```

```python
import functools
import math

import jax
import jax.numpy as jnp
from jax import lax
from jax.experimental import pallas as pl
from jax.experimental.pallas import tpu as pltpu

D_MODEL = 1024
D_INNER = 2048
CHUNK = 64
SG_BLOCK = 128
SG_GROUPS = 8
SG_GROUP_DIM = D_INNER // SG_GROUPS
HG_HEADS = 16
HG_HEAD_DIM = D_INNER // HG_HEADS
HEAD_PAIRS = HG_HEADS // 2
PAIR_DIM = 2 * HG_HEAD_DIM
EPS = 1e-6

TOKENS_PER_STEP = 256
TILE_ROWS = 256
HGRN2_TILES_PER_STEP = 1
PROJ_SLOTS = 2
VMEM_LIMIT_BYTES = 56 * 1024 * 1024

_BF16 = jnp.bfloat16
_F32 = jnp.float32


def _dot(a, b):
    return jnp.dot(a, b, preferred_element_type=_F32)


def _sigmoid(z):
    return 1.0 / (1.0 + jnp.exp(-z))


def _silu(z):
    return z * _sigmoid(z)


def _gelu_tanh(z):
    c = math.sqrt(2.0 / math.pi)
    return 0.5 * z * (1.0 + jnp.tanh(c * (z + 0.044715 * (z * z * z))))


def _modulated_norm(x, gain, mod_ref):
    r = lax.rsqrt(jnp.mean(x * x, axis=-1, keepdims=True) + EPS)
    shift = mod_ref[0, 0:1, :]
    scale = mod_ref[0, 1:2, :]
    return (x * r) * (gain * (1.0 + scale)) + shift


def _stage_weight_blocks(src_hbm, dst_ref, stage_ref, sem_ref, blocks):
    def block_copy(j):
        return pltpu.make_async_copy(src_hbm.at[blocks[j][0]], stage_ref.at[j % 2], sem_ref.at[j % 2])

    block_copy(0).start()
    for j in range(len(blocks)):
        if j + 1 < len(blocks):
            block_copy(j + 1).start()
        block_copy(j).wait()
        dst_ref[blocks[j][1]] = stage_ref[j % 2].astype(_BF16)


def _first_grid_step():
    return (pl.program_id(0) == 0) & (pl.program_id(1) == 0)


def _adaln_kernel(c_ref, w_ref, b_ref, o_ref):
    c_act = _silu(c_ref[...]).astype(_BF16)
    o_ref[0] = _dot(c_act, w_ref[0].astype(_BF16)) + b_ref[0]


def _adaln_mod(c, w_ada, b_ada):
    depth, d, n3 = w_ada.shape
    bsz = c.shape[0]
    nt = n3 // d
    return pl.pallas_call(
        _adaln_kernel,
        grid=(depth, nt),
        in_specs=[
            pl.BlockSpec((bsz, d), lambda l, j: (0, 0)),
            pl.BlockSpec((1, d, d), lambda l, j: (l, 0, j)),
            pl.BlockSpec((1, 1, d), lambda l, j: (l, 0, j)),
        ],
        out_specs=pl.BlockSpec((1, bsz, d), lambda l, j: (l, 0, j)),
        out_shape=jax.ShapeDtypeStruct((depth, bsz, n3), _F32),
        compiler_params=pltpu.CompilerParams(dimension_semantics=("arbitrary", "arbitrary")),
        name="adaln_mod",
    )(c, w_ada, b_ada.reshape(depth, 1, n3))


def _sgu_kernel(x_ref, mod_ref, gain_ref, win_hbm, lng_ref, lnb_ref, ws_ref, bs_ref, wout_hbm,
                o_ref, h_ref, v_ref, ug_ref, acc_ref, win_ref, wout_ref, win_stage, wout_stage, sems):
    tm = x_ref.shape[1]
    U, V, GATE = range(3)

    @pl.when(_first_grid_step())
    def _():
        _stage_weight_blocks(
            win_hbm, win_ref, win_stage, sems.at[0],
            [((0, slice(None), pl.ds(p * D_INNER + g * SG_GROUP_DIM, SG_GROUP_DIM)), (p, g))
             for p in range(3) for g in range(SG_GROUPS)])
        _stage_weight_blocks(
            wout_hbm, wout_ref, wout_stage, sems.at[1],
            [((0, pl.ds(g * SG_GROUP_DIM, SG_GROUP_DIM), slice(None)), (g,)) for g in range(SG_GROUPS)])

    x = x_ref[0]
    h_ref[...] = _modulated_norm(x, gain_ref[...], mod_ref).astype(_BF16)

    t_chunk = lax.broadcasted_iota(jnp.int32, (SG_BLOCK, SG_BLOCK), 0) // CHUNK
    s_chunk = lax.broadcasted_iota(jnp.int32, (SG_BLOCK, SG_BLOCK), 1) // CHUNK
    allowed = s_chunk <= t_chunk

    def group_cols(g):
        return slice(g * SG_GROUP_DIM, (g + 1) * SG_GROUP_DIM)

    def project_ug(g):
        ug_ref[g % 2, 0] = _dot(h_ref[...], win_ref[U, g])
        ug_ref[g % 2, 1] = _dot(h_ref[...], win_ref[GATE, g])

    def lane_halves_sum(z):
        return z[:, :SG_GROUP_DIM // 2] + z[:, SG_GROUP_DIM // 2:]

    row_sum = None
    for g in range(SG_GROUPS):
        vg = _gelu_tanh(_dot(h_ref[...], win_ref[V, g]))
        v_ref[g] = vg
        row_sum = lane_halves_sum(vg) if row_sum is None else row_sum + lane_halves_sum(vg)
    mu = jnp.sum(row_sum, axis=-1, keepdims=True) * (1.0 / D_INNER)
    project_ug(0)
    sq_sum = None
    for g in range(SG_GROUPS):
        vc = v_ref[g] - mu
        sq_sum = lane_halves_sum(vc * vc) if sq_sum is None else sq_sum + lane_halves_sum(vc * vc)
    var = jnp.sum(sq_sum, axis=-1, keepdims=True) * (1.0 / D_INNER)
    rstd = lax.rsqrt(var + EPS)

    for g in range(SG_GROUPS):
        vn = ((v_ref[g] - mu) * rstd * lng_ref[:, group_cols(g)] + lnb_ref[:, group_cols(g)]).astype(_BF16)
        ws = jnp.where(allowed, ws_ref[g], 0.0).astype(_BF16)
        bias = bs_ref[g]
        s = jnp.concatenate(
            [_dot(ws, vn[blk * SG_BLOCK:(blk + 1) * SG_BLOCK, :]) + bias for blk in range(tm // SG_BLOCK)],
            axis=0)
        if g + 1 < SG_GROUPS:
            project_ug(g + 1)
        u = _gelu_tanh(ug_ref[g % 2, 0])
        gate = _silu(ug_ref[g % 2, 1])
        y = (u * s * gate).astype(_BF16)
        contrib = _dot(y, wout_ref[g])
        if g == 0:
            acc_ref[...] = contrib
        else:
            acc_ref[...] += contrib

    o_ref[0] = x + mod_ref[0, 2:3, :] * acc_ref[...]


def _sgu_layer(x, mod, gain, w_in, ln_gain, ln_bias, w_s, b_s, w_out):
    bsz, seq, d = x.shape
    tm = TOKENS_PER_STEP
    const = lambda nd: (lambda b, t: (0,) * nd)
    resident = functools.partial(pl.BlockSpec, pipeline_mode=pl.Buffered(1))
    in_hbm = pl.BlockSpec(memory_space=pl.ANY)
    return pl.pallas_call(
        _sgu_kernel,
        grid=(bsz, seq // tm),
        in_specs=[
            pl.BlockSpec((1, tm, d), lambda b, t: (b, t, 0)),
            pl.BlockSpec((1, 3, d), lambda b, t: (b, 0, 0)),
            resident((1, d), const(2)),
            in_hbm,
            resident((1, D_INNER), const(2)),
            resident((1, D_INNER), const(2)),
            resident(w_s.shape, const(3)),
            resident(b_s.shape, const(3)),
            in_hbm,
        ],
        out_specs=pl.BlockSpec((1, tm, d), lambda b, t: (b, t, 0)),
        out_shape=jax.ShapeDtypeStruct(x.shape, x.dtype),
        scratch_shapes=[
            pltpu.VMEM((tm, d), _BF16),
            pltpu.VMEM((SG_GROUPS, tm, SG_GROUP_DIM), _F32),
            pltpu.VMEM((2, 2, tm, SG_GROUP_DIM), _F32),
            pltpu.VMEM((tm, d), _F32),
            pltpu.VMEM((3, SG_GROUPS, d, SG_GROUP_DIM), _BF16),
            pltpu.VMEM((SG_GROUPS, SG_GROUP_DIM, d), _BF16),
            pltpu.VMEM((2, d, SG_GROUP_DIM), _F32),
            pltpu.VMEM((2, SG_GROUP_DIM, d), _F32),
            pltpu.SemaphoreType.DMA((2, 2)),
        ],
        compiler_params=pltpu.CompilerParams(
            dimension_semantics=("arbitrary", "arbitrary"), vmem_limit_bytes=VMEM_LIMIT_BYTES),
        name="sgu_layer",
    )(x, mod, gain, w_in, ln_gain, ln_bias, w_s, b_s, w_out)


def _hgrn2_kernel(x_ref, mod_ref, gain_ref, w_hbm, lbraw_ref, gn_ref, wout_hbm, fgain_ref,
                  o_ref, h_ref, acc_ref, state_ref, proj_ref, w_ref, wout_ref, w_stage, wout_stage, sems,
                  *, layer):
    tm = TILE_ROWS
    n_tiles = x_ref.shape[1] // tm
    n_items = n_tiles * HEAD_PAIRS
    n_chunks = tm // CHUNK
    Q, F, I, G = range(4)

    @pl.when(_first_grid_step())
    def _():
        _stage_weight_blocks(
            w_hbm, w_ref, w_stage, sems.at[0],
            [((0, slice(None), pl.ds(part * D_INNER + hp * PAIR_DIM, PAIR_DIM)), (hp, part))
             for part in range(4) for hp in range(HEAD_PAIRS)])
        _stage_weight_blocks(
            wout_hbm, wout_ref, wout_stage, sems.at[1],
            [((0, pl.ds(hp * PAIR_DIM, PAIR_DIM), pl.ds(n * PAIR_DIM, PAIR_DIM)), (hp, n))
             for hp in range(HEAD_PAIRS) for n in range(D_MODEL // PAIR_DIM)])

    @pl.when(pl.program_id(1) == 0)
    def _():
        state_ref[...] = jnp.zeros_like(state_ref)

    row = lax.broadcasted_iota(jnp.int32, (tm, tm), 0)
    col = lax.broadcasted_iota(jnp.int32, (tm, tm), 1)
    intra = (row // CHUNK == col // CHUNK) & (col <= row)
    step_in_chunk = lax.broadcasted_iota(jnp.int32, (tm, PAIR_DIM), 0) % CHUNK
    gn = gn_ref[...]
    heads = [slice(hd * HG_HEAD_DIM, (hd + 1) * HG_HEAD_DIM) for hd in range(2)]
    chunks = [slice(c * CHUNK, (c + 1) * CHUNK) for c in range(n_chunks)]
    row_halves = [slice(0, tm // 2), slice(tm // 2, tm)]

    def tile_rows(tile):
        return slice(tile * tm, (tile + 1) * tm)

    def normalize_tile(tile):
        h_ref[tile] = _modulated_norm(x_ref[0, tile_rows(tile)], gain_ref[...], mod_ref).astype(_BF16)

    def finish_tile(tile):
        xo = x_ref[0, tile_rows(tile)] + mod_ref[0, 2:3, :] * acc_ref[tile]
        r = lax.rsqrt(jnp.mean(xo * xo, axis=-1, keepdims=True) + EPS)
        o_ref[0, tile_rows(tile)] = xo * r * fgain_ref[...]

    def project(it, part):
        tile, hp = divmod(it, HEAD_PAIRS)
        for rows in row_halves:
            proj_ref[it % PROJ_SLOTS, part, rows] = _dot(h_ref[tile, rows], w_ref[hp, part])

    def forget_gate(it):
        hp = it % HEAD_PAIRS
        lbraw = lbraw_ref[hp]
        e = jnp.exp(lbraw - jnp.max(lbraw, axis=0, keepdims=True))
        p = e / jnp.sum(e, axis=0, keepdims=True)
        lb = jnp.sum(p[1:layer + 1], axis=0, keepdims=True)
        f = lb + (1.0 - lb) * _sigmoid(proj_ref[it % PROJ_SLOTS, F])
        return 1.0 - f, jnp.log(f)

    def prefix_sum(z):
        shift = 1
        while shift < CHUNK:
            z = z + jnp.where(step_in_chunk >= shift, pltpu.roll(z, shift, axis=0), 0.0)
            shift *= 2
        return z

    def decay_scaling(it, k, a):
        def chunk_rows(offset):
            return jnp.concatenate(
                [jnp.broadcast_to(a[c * CHUNK + offset:c * CHUNK + offset + 1, :], (CHUNK, PAIR_DIM))
                 for c in range(n_chunks)], axis=0)
        a_mid = chunk_rows(CHUNK // 2 - 1)
        a_end = chunk_rows(CHUNK - 1)
        q = _silu(proj_ref[it % PROJ_SLOTS, Q])
        q_in = q * jnp.exp(a - a_mid)
        k_in = k * jnp.exp(a_mid - a)
        q_out = (q_in * jnp.exp(a_mid)).astype(_BF16)
        k_out = (k_in * jnp.exp(a_end - a_mid)).astype(_BF16)
        decay = [jnp.exp(a[c * CHUNK + CHUNK - 1:c * CHUNK + CHUNK, :]) for c in range(n_chunks)]
        k_in_t = k_in.T.astype(_BF16)
        return q_in.astype(_BF16), k_in_t, q_out, k_out, decay

    def scores_and_updates(it, q_in, k_in_t, k_out):
        vv32 = proj_ref[it % PROJ_SLOTS, I]
        vv = vv32.astype(_BF16)
        vv_t = vv32.T.astype(_BF16)
        scores = [_dot(q_in[:, ln], k_in_t[ln, :]) for ln in heads]
        kv = []
        for ln in heads:
            blocks = []
            for c, rw in enumerate(chunks):
                pieces = [k_out[rw, ln]]
                if c > 0:
                    pieces.insert(0, jnp.zeros((c * CHUNK, HG_HEAD_DIM), _BF16))
                if c + 1 < n_chunks:
                    pieces.append(jnp.zeros(((n_chunks - 1 - c) * CHUNK, HG_HEAD_DIM), _BF16))
                blocks.append(jnp.concatenate(pieces, axis=0))
            kv.append(_dot(vv_t[ln, :], jnp.concatenate(blocks, axis=1)))
        return vv, scores, kv

    def mask_and_states(it, scores, kv, decay):
        hp = it % HEAD_PAIRS
        scores = [jnp.where(intra, s, 0.0).astype(_BF16) for s in scores]
        states = []
        for hd, ln in enumerate(heads):
            st = state_ref[hp * 2 + hd]
            per_chunk = []
            for c in range(n_chunks):
                per_chunk.append(st.T.astype(_BF16))
                st = decay[c][:, ln] * st + kv[hd][:, c * HG_HEAD_DIM:(c + 1) * HG_HEAD_DIM]
            state_ref[hp * 2 + hd] = st
            states.append(per_chunk)
        return scores, states

    def outputs(vv, scores, q_out, states):
        o_heads = []
        for hd, ln in enumerate(heads):
            o_chunks = []
            for c, rw in enumerate(chunks):
                blk = slice((c // 2) * SG_BLOCK, (c // 2 + 1) * SG_BLOCK)
                lhs = jnp.concatenate([q_out[rw, ln], scores[hd][rw, blk]], axis=1)
                rhs = jnp.concatenate([states[hd][c], vv[blk, ln]], axis=0)
                o_chunks.append(_dot(lhs, rhs))
            o_heads.append(jnp.concatenate(o_chunks, axis=0))
        return o_heads

    def gated_norm(it, o_heads):
        og = _silu(proj_ref[it % PROJ_SLOTS, G])
        normed = [o * lax.rsqrt(jnp.mean(o * o, axis=-1, keepdims=True) + EPS) * gn for o in o_heads]
        return (jnp.concatenate(normed, axis=1) * og).astype(_BF16)

    def out_project(it, y):
        tile, hp = divmod(it, HEAD_PAIRS)
        for n in range(D_MODEL // PAIR_DIM):
            cols = slice(n * PAIR_DIM, (n + 1) * PAIR_DIM)
            contrib = _dot(y, wout_ref[hp, n])
            if hp == 0:
                acc_ref[tile, :, cols] = contrib
            else:
                acc_ref[tile, :, cols] += contrib
        if hp == HEAD_PAIRS - 1:
            finish_tile(tile)

    normalize_tile(0)
    for part in (F, Q, I, G):
        project(0, part)
    y_prev = None
    for it in range(n_items):
        tile, hp = divmod(it, HEAD_PAIRS)
        nxt = it + 1 if it + 1 < n_items else None
        if hp == HEAD_PAIRS // 2 and tile + 1 < n_tiles:
            normalize_tile(tile + 1)
        k, log_f = forget_gate(it)
        a = prefix_sum(log_f)
        if y_prev is not None:
            out_project(it - 1, y_prev)
        if nxt is not None:
            project(nxt, F)
        q_in, k_in_t, q_out, k_out, decay = decay_scaling(it, k, a)
        vv, scores, kv = scores_and_updates(it, q_in, k_in_t, k_out)
        if nxt is not None:
            project(nxt, Q)
        scores, states = mask_and_states(it, scores, kv, decay)
        o_heads = outputs(vv, scores, q_out, states)
        if nxt is not None:
            project(nxt, I)
            project(nxt, G)
        y_prev = gated_norm(it, o_heads)
    out_project(n_items - 1, y_prev)


def _hgrn2_layer(x, mod, gain, w, lb_raw, gn_gain, w_out, final_gain, layer):
    bsz, seq, d = x.shape
    tm = HGRN2_TILES_PER_STEP * TILE_ROWS
    const = lambda nd: (lambda b, t: (0,) * nd)
    resident = functools.partial(pl.BlockSpec, pipeline_mode=pl.Buffered(1))
    in_hbm = pl.BlockSpec(memory_space=pl.ANY)
    return pl.pallas_call(
        functools.partial(_hgrn2_kernel, layer=layer),
        grid=(bsz, seq // tm),
        in_specs=[
            pl.BlockSpec((1, tm, d), lambda b, t: (b, t, 0)),
            pl.BlockSpec((1, 3, d), lambda b, t: (b, 0, 0)),
            resident((1, d), const(2)),
            in_hbm,
            resident(lb_raw.shape, const(3)),
            resident((1, HG_HEAD_DIM), const(2)),
            in_hbm,
            resident((1, d), const(2)),
        ],
        out_specs=pl.BlockSpec((1, tm, d), lambda b, t: (b, t, 0)),
        out_shape=jax.ShapeDtypeStruct(x.shape, x.dtype),
        scratch_shapes=[
            pltpu.VMEM((HGRN2_TILES_PER_STEP, TILE_ROWS, d), _BF16),
            pltpu.VMEM((HGRN2_TILES_PER_STEP, TILE_ROWS, d), _F32),
            pltpu.VMEM((HG_HEADS, HG_HEAD_DIM, HG_HEAD_DIM), _F32),
            pltpu.VMEM((PROJ_SLOTS, 4, TILE_ROWS, PAIR_DIM), _F32),
            pltpu.VMEM((HEAD_PAIRS, 4, d, PAIR_DIM), _BF16),
            pltpu.VMEM((HEAD_PAIRS, d // PAIR_DIM, PAIR_DIM, PAIR_DIM), _BF16),
            pltpu.VMEM((2, d, PAIR_DIM), _F32),
            pltpu.VMEM((2, PAIR_DIM, PAIR_DIM), _F32),
            pltpu.SemaphoreType.DMA((2, 2)),
        ],
        compiler_params=pltpu.CompilerParams(
            dimension_semantics=("arbitrary", "arbitrary"), vmem_limit_bytes=VMEM_LIMIT_BYTES),
        name="hgrn2_layer",
    )(x, mod, gain, w, lb_raw, gn_gain, w_out, final_gain)


def kernel(x, c, norm_gain, w_ada, b_ada, a_w_in, a_ln_gain, a_ln_bias, a_w_s, a_b_s, a_w_out,
           b_w_in, b_lower_bounds, b_gn_gain, b_w_out, final_gain):
    bsz, seq, d = x.shape
    depth = norm_gain.shape[0]
    assert depth == 2 and a_w_in.shape[0] == 1 and b_w_in.shape[0] == 1
    assert seq % TOKENS_PER_STEP == 0 and TOKENS_PER_STEP % SG_BLOCK == 0
    assert seq % (HGRN2_TILES_PER_STEP * TILE_ROWS) == 0

    mod = _adaln_mod(c, w_ada, b_ada)
    mod = mod.reshape(depth, bsz, 3, d)

    x1 = _sgu_layer(
        x, mod[0], norm_gain[0:1], a_w_in, a_ln_gain, a_ln_bias, a_w_s[0],
        a_b_s[0].reshape(SG_GROUPS, SG_BLOCK, 1), a_w_out)

    lb_raw = b_lower_bounds.reshape(depth, HEAD_PAIRS, PAIR_DIM).transpose(1, 0, 2)
    return _hgrn2_layer(
        x1, mod[1], norm_gain[1:2], b_w_in, lb_raw, b_gn_gain, b_w_out, final_gain.reshape(1, d),
        layer=1)
```

```python
import functools
import math

import jax
import jax.numpy as jnp
from jax import lax
from jax.experimental import pallas as pl
from jax.experimental.pallas import tpu as pltpu

D_MODEL = 1024
D_INNER = 2048
CHUNK = 64
SG_BLOCK = 128
SG_GROUPS = 8
SG_GROUP_DIM = D_INNER // SG_GROUPS
HG_HEADS = 16
HG_HEAD_DIM = D_INNER // HG_HEADS
HEAD_PAIRS = HG_HEADS // 2
PAIR_DIM = 2 * HG_HEAD_DIM
EPS = 1e-6

TOKENS_PER_STEP = 256
TILE_ROWS = 256
HGRN2_TILES_PER_STEP = 1
PROJ_SLOTS = 2
STAGE_SLOTS = 4
VMEM_LIMIT_BYTES = 56 * 1024 * 1024

_BF16 = jnp.bfloat16
_F32 = jnp.float32


def _dot(a, b):
    return jnp.dot(a, b, preferred_element_type=_F32)


def _sigmoid(z):
    return 1.0 / (1.0 + jnp.exp(-z))


def _silu(z):
    return z * _sigmoid(z)


def _gelu_tanh(z):
    c = math.sqrt(2.0 / math.pi)
    return 0.5 * z * (1.0 + jnp.tanh(c * (z + 0.044715 * (z * z * z))))


def _modulated_norm(x, gain, mod_ref):
    r = lax.rsqrt(jnp.mean(x * x, axis=-1, keepdims=True) + EPS)
    shift = mod_ref[0, 0:1, :]
    scale = mod_ref[0, 1:2, :]
    return (x * r) * (gain * (1.0 + scale)) + shift


def _stage_weight_blocks(src_hbm, dst_ref, stage_ref, sem_ref, blocks):
    slots = stage_ref.shape[0]

    def block_copy(j):
        return pltpu.make_async_copy(
            src_hbm.at[blocks[j][0]], stage_ref.at[j % slots], sem_ref.at[j % slots])

    for j in range(min(slots - 1, len(blocks))):
        block_copy(j).start()
    for j in range(len(blocks)):
        if j + slots - 1 < len(blocks):
            block_copy(j + slots - 1).start()
        block_copy(j).wait()
        dst_ref[blocks[j][1]] = stage_ref[j % slots].astype(_BF16)


def _first_grid_step():
    return (pl.program_id(0) == 0) & (pl.program_id(1) == 0)


def _adaln_kernel(c_ref, w_ref, b_ref, o_ref):
    c_act = _silu(c_ref[...]).astype(_BF16)
    o_ref[0] = _dot(c_act, w_ref[0].astype(_BF16)) + b_ref[0]


def _adaln_mod(c, w_ada, b_ada):
    depth, d, n3 = w_ada.shape
    bsz = c.shape[0]
    nt = n3 // d
    return pl.pallas_call(
        _adaln_kernel,
        grid=(depth, nt),
        in_specs=[
            pl.BlockSpec((bsz, d), lambda l, j: (0, 0)),
            pl.BlockSpec((1, d, d), lambda l, j: (l, 0, j)),
            pl.BlockSpec((1, 1, d), lambda l, j: (l, 0, j)),
        ],
        out_specs=pl.BlockSpec((1, bsz, d), lambda l, j: (l, 0, j)),
        out_shape=jax.ShapeDtypeStruct((depth, bsz, n3), _F32),
        compiler_params=pltpu.CompilerParams(dimension_semantics=("arbitrary", "arbitrary")),
        name="adaln_mod",
    )(c, w_ada, b_ada.reshape(depth, 1, n3))


def _sgu_kernel(x_ref, mod_ref, gain_ref, win_hbm, lng_ref, lnb_ref, ws_ref, bs_ref, wout_hbm,
                o_ref, h_ref, v_ref, ug_ref, acc_ref, win_ref, wout_ref, win_stage, wout_stage, win_sems, wout_sems):
    tm = x_ref.shape[1]
    U, V, GATE = range(3)

    @pl.when(_first_grid_step())
    def _():
        _stage_weight_blocks(
            win_hbm, win_ref, win_stage, win_sems,
            [((0, pl.ds(0, D_MODEL), pl.ds(p * D_INNER + g * SG_GROUP_DIM, SG_GROUP_DIM)), (p, g))
             for p in range(3) for g in range(SG_GROUPS)])
        _stage_weight_blocks(
            wout_hbm, wout_ref, wout_stage, wout_sems,
            [((0, pl.ds(g * SG_GROUP_DIM, SG_GROUP_DIM), pl.ds(0, D_MODEL)), (g,)) for g in range(SG_GROUPS)])

    x = x_ref[0]
    h_ref[...] = _modulated_norm(x, gain_ref[...], mod_ref).astype(_BF16)

    t_chunk = lax.broadcasted_iota(jnp.int32, (SG_BLOCK, SG_BLOCK), 0) // CHUNK
    s_chunk = lax.broadcasted_iota(jnp.int32, (SG_BLOCK, SG_BLOCK), 1) // CHUNK
    allowed = s_chunk <= t_chunk

    def group_cols(g):
        return slice(g * SG_GROUP_DIM, (g + 1) * SG_GROUP_DIM)

    def project_ug(g):
        ug_ref[g % 2, 0] = _dot(h_ref[...], win_ref[U, g])
        ug_ref[g % 2, 1] = _dot(h_ref[...], win_ref[GATE, g])

    def lane_halves_sum(z):
        return z[:, :SG_GROUP_DIM // 2] + z[:, SG_GROUP_DIM // 2:]

    row_sum = None
    for g in range(SG_GROUPS):
        vg = _gelu_tanh(_dot(h_ref[...], win_ref[V, g]))
        v_ref[g] = vg
        row_sum = lane_halves_sum(vg) if row_sum is None else row_sum + lane_halves_sum(vg)
    mu = jnp.sum(row_sum, axis=-1, keepdims=True) * (1.0 / D_INNER)
    project_ug(0)
    sq_sum = None
    for g in range(SG_GROUPS):
        vc = v_ref[g] - mu
        sq_sum = lane_halves_sum(vc * vc) if sq_sum is None else sq_sum + lane_halves_sum(vc * vc)
    var = jnp.sum(sq_sum, axis=-1, keepdims=True) * (1.0 / D_INNER)
    rstd = lax.rsqrt(var + EPS)

    for g in range(SG_GROUPS):
        vn = ((v_ref[g] - mu) * rstd * lng_ref[:, group_cols(g)] + lnb_ref[:, group_cols(g)]).astype(_BF16)
        ws = jnp.where(allowed, ws_ref[g], 0.0).astype(_BF16)
        bias = bs_ref[g]
        s = jnp.concatenate(
            [_dot(ws, vn[blk * SG_BLOCK:(blk + 1) * SG_BLOCK, :]) + bias for blk in range(tm // SG_BLOCK)],
            axis=0)
        if g + 1 < SG_GROUPS:
            project_ug(g + 1)
        u = _gelu_tanh(ug_ref[g % 2, 0])
        gate = _silu(ug_ref[g % 2, 1])
        y = (u * s * gate).astype(_BF16)
        contrib = _dot(y, wout_ref[g])
        if g == 0:
            acc_ref[...] = contrib
        else:
            acc_ref[...] += contrib

    o_ref[0] = x + mod_ref[0, 2:3, :] * acc_ref[...]


def _sgu_layer(x, mod, gain, w_in, ln_gain, ln_bias, w_s, b_s, w_out):
    bsz, seq, d = x.shape
    tm = TOKENS_PER_STEP
    const = lambda nd: (lambda b, t: (0,) * nd)
    resident = functools.partial(pl.BlockSpec, pipeline_mode=pl.Buffered(1))
    in_hbm = pl.BlockSpec(memory_space=pl.ANY)
    return pl.pallas_call(
        _sgu_kernel,
        grid=(bsz, seq // tm),
        in_specs=[
            pl.BlockSpec((1, tm, d), lambda b, t: (b, t, 0)),
            pl.BlockSpec((1, 3, d), lambda b, t: (b, 0, 0)),
            resident((1, d), const(2)),
            in_hbm,
            resident((1, D_INNER), const(2)),
            resident((1, D_INNER), const(2)),
            resident(w_s.shape, const(3)),
            resident(b_s.shape, const(3)),
            in_hbm,
        ],
        out_specs=pl.BlockSpec((1, tm, d), lambda b, t: (b, t, 0)),
        out_shape=jax.ShapeDtypeStruct(x.shape, x.dtype),
        scratch_shapes=[
            pltpu.VMEM((tm, d), _BF16),
            pltpu.VMEM((SG_GROUPS, tm, SG_GROUP_DIM), _F32),
            pltpu.VMEM((2, 2, tm, SG_GROUP_DIM), _F32),
            pltpu.VMEM((tm, d), _F32),
            pltpu.VMEM((3, SG_GROUPS, d, SG_GROUP_DIM), _BF16),
            pltpu.VMEM((SG_GROUPS, SG_GROUP_DIM, d), _BF16),
            pltpu.VMEM((STAGE_SLOTS, d, SG_GROUP_DIM), _F32),
            pltpu.VMEM((STAGE_SLOTS, SG_GROUP_DIM, d), _F32),
            pltpu.SemaphoreType.DMA((STAGE_SLOTS,)),
            pltpu.SemaphoreType.DMA((STAGE_SLOTS,)),
        ],
        compiler_params=pltpu.CompilerParams(
            dimension_semantics=("arbitrary", "arbitrary"), vmem_limit_bytes=VMEM_LIMIT_BYTES),
        name="sgu_layer",
    )(x, mod, gain, w_in, ln_gain, ln_bias, w_s, b_s, w_out)


def _hgrn2_kernel(x_ref, mod_ref, gain_ref, w_hbm, lbraw_ref, gn_ref, wout_hbm, fgain_ref,
                  o_ref, h_ref, acc_ref, state_ref, proj_ref, w_ref, wout_ref, w_stage, wout_stage, w_sems, wout_sems,
                  *, layer):
    tm = TILE_ROWS
    n_tiles = x_ref.shape[1] // tm
    n_items = n_tiles * HEAD_PAIRS
    n_chunks = tm // CHUNK
    Q, F, I, G = range(4)

    @pl.when(_first_grid_step())
    def _():
        _stage_weight_blocks(
            w_hbm, w_ref, w_stage, w_sems,
            [((0, pl.ds(0, D_MODEL), pl.ds(part * D_INNER + hp * PAIR_DIM, PAIR_DIM)), (hp, part))
             for part in range(4) for hp in range(HEAD_PAIRS)])
        _stage_weight_blocks(
            wout_hbm, wout_ref, wout_stage, wout_sems,
            [((0, pl.ds(hp * PAIR_DIM, PAIR_DIM), pl.ds(n * PAIR_DIM, PAIR_DIM)), (hp, n))
             for hp in range(HEAD_PAIRS) for n in range(D_MODEL // PAIR_DIM)])

    @pl.when(pl.program_id(1) == 0)
    def _():
        state_ref[...] = jnp.zeros_like(state_ref)

    row = lax.broadcasted_iota(jnp.int32, (tm, tm), 0)
    col = lax.broadcasted_iota(jnp.int32, (tm, tm), 1)
    intra = (row // CHUNK == col // CHUNK) & (col <= row)
    step_in_chunk = lax.broadcasted_iota(jnp.int32, (tm, PAIR_DIM), 0) % CHUNK
    gn = gn_ref[...]
    heads = [slice(hd * HG_HEAD_DIM, (hd + 1) * HG_HEAD_DIM) for hd in range(2)]
    chunks = [slice(c * CHUNK, (c + 1) * CHUNK) for c in range(n_chunks)]
    row_halves = [slice(0, tm // 2), slice(tm // 2, tm)]

    def tile_rows(tile):
        return slice(tile * tm, (tile + 1) * tm)

    def normalize_tile(tile):
        h_ref[tile] = _modulated_norm(x_ref[0, tile_rows(tile)], gain_ref[...], mod_ref).astype(_BF16)

    def finish_tile(tile):
        xo = x_ref[0, tile_rows(tile)] + mod_ref[0, 2:3, :] * acc_ref[tile]
        r = lax.rsqrt(jnp.mean(xo * xo, axis=-1, keepdims=True) + EPS)
        o_ref[0, tile_rows(tile)] = xo * r * fgain_ref[...]

    def project(it, part):
        tile, hp = divmod(it, HEAD_PAIRS)
        for rows in row_halves:
            proj_ref[it % PROJ_SLOTS, part, rows] = _dot(h_ref[tile, rows], w_ref[hp, part])

    def forget_gate(it):
        hp = it % HEAD_PAIRS
        lbraw = lbraw_ref[hp]
        e = jnp.exp(lbraw - jnp.max(lbraw, axis=0, keepdims=True))
        p = e / jnp.sum(e, axis=0, keepdims=True)
        lb = jnp.sum(p[1:layer + 1], axis=0, keepdims=True)
        f = lb + (1.0 - lb) * _sigmoid(proj_ref[it % PROJ_SLOTS, F])
        return 1.0 - f, jnp.log(f)

    def prefix_sum(z):
        shift = 1
        while shift < CHUNK:
            z = z + jnp.where(step_in_chunk >= shift, pltpu.roll(z, shift, axis=0), 0.0)
            shift *= 2
        return z

    def decay_scaling(it, k, a):
        def chunk_rows(offset):
            return jnp.concatenate(
                [jnp.broadcast_to(a[c * CHUNK + offset:c * CHUNK + offset + 1, :], (CHUNK, PAIR_DIM))
                 for c in range(n_chunks)], axis=0)
        a_mid = chunk_rows(CHUNK // 2 - 1)
        a_end = chunk_rows(CHUNK - 1)
        q = _silu(proj_ref[it % PROJ_SLOTS, Q])
        q_in = q * jnp.exp(a - a_mid)
        k_in = k * jnp.exp(a_mid - a)
        q_out = (q_in * jnp.exp(a_mid)).astype(_BF16)
        k_out = (k_in * jnp.exp(a_end - a_mid)).astype(_BF16)
        decay = [jnp.exp(a[c * CHUNK + CHUNK - 1:c * CHUNK + CHUNK, :]) for c in range(n_chunks)]
        k_in_t = k_in.T.astype(_BF16)
        return q_in.astype(_BF16), k_in_t, q_out, k_out, decay

    def scores_and_updates(it, q_in, k_in_t, k_out):
        vv32 = proj_ref[it % PROJ_SLOTS, I]
        vv = vv32.astype(_BF16)
        vv_t = vv32.T.astype(_BF16)
        scores = [_dot(q_in[:, ln], k_in_t[ln, :]) for ln in heads]
        kv = []
        for ln in heads:
            blocks = []
            for c, rw in enumerate(chunks):
                pieces = [k_out[rw, ln]]
                if c > 0:
                    pieces.insert(0, jnp.zeros((c * CHUNK, HG_HEAD_DIM), _BF16))
                if c + 1 < n_chunks:
                    pieces.append(jnp.zeros(((n_chunks - 1 - c) * CHUNK, HG_HEAD_DIM), _BF16))
                blocks.append(jnp.concatenate(pieces, axis=0))
            kv.append(_dot(vv_t[ln, :], jnp.concatenate(blocks, axis=1)))
        return vv, scores, kv

    def mask_and_states(it, scores, kv, decay):
        hp = it % HEAD_PAIRS
        scores = [jnp.where(intra, s, 0.0).astype(_BF16) for s in scores]
        states = []
        for hd, ln in enumerate(heads):
            st = state_ref[hp * 2 + hd]
            per_chunk = []
            for c in range(n_chunks):
                per_chunk.append(st.T.astype(_BF16))
                st = decay[c][:, ln] * st + kv[hd][:, c * HG_HEAD_DIM:(c + 1) * HG_HEAD_DIM]
            state_ref[hp * 2 + hd] = st
            states.append(per_chunk)
        return scores, states

    def outputs(vv, scores, q_out, states):
        o_heads = []
        for hd, ln in enumerate(heads):
            o_chunks = []
            for c, rw in enumerate(chunks):
                blk = slice((c // 2) * SG_BLOCK, (c // 2 + 1) * SG_BLOCK)
                lhs = jnp.concatenate([q_out[rw, ln], scores[hd][rw, blk]], axis=1)
                rhs = jnp.concatenate([states[hd][c], vv[blk, ln]], axis=0)
                o_chunks.append(_dot(lhs, rhs))
            o_heads.append(jnp.concatenate(o_chunks, axis=0))
        return o_heads

    def gated_norm(it, o_heads):
        og = _silu(proj_ref[it % PROJ_SLOTS, G])
        normed = [o * lax.rsqrt(jnp.mean(o * o, axis=-1, keepdims=True) + EPS) * gn for o in o_heads]
        return (jnp.concatenate(normed, axis=1) * og).astype(_BF16)

    def out_project(it, y):
        tile, hp = divmod(it, HEAD_PAIRS)
        for n in range(D_MODEL // PAIR_DIM):
            cols = slice(n * PAIR_DIM, (n + 1) * PAIR_DIM)
            contrib = _dot(y, wout_ref[hp, n])
            if hp == 0:
                acc_ref[tile, :, cols] = contrib
            else:
                acc_ref[tile, :, cols] += contrib
        if hp == HEAD_PAIRS - 1:
            finish_tile(tile)

    normalize_tile(0)
    for part in (F, Q, I, G):
        project(0, part)
    y_prev = None
    for it in range(n_items):
        tile, hp = divmod(it, HEAD_PAIRS)
        nxt = it + 1 if it + 1 < n_items else None
        if hp == HEAD_PAIRS // 2 and tile + 1 < n_tiles:
            normalize_tile(tile + 1)
        k, log_f = forget_gate(it)
        a = prefix_sum(log_f)
        if y_prev is not None:
            out_project(it - 1, y_prev)
        if nxt is not None:
            project(nxt, F)
        q_in, k_in_t, q_out, k_out, decay = decay_scaling(it, k, a)
        vv, scores, kv = scores_and_updates(it, q_in, k_in_t, k_out)
        if nxt is not None:
            project(nxt, Q)
        scores, states = mask_and_states(it, scores, kv, decay)
        o_heads = outputs(vv, scores, q_out, states)
        if nxt is not None:
            project(nxt, I)
            project(nxt, G)
        y_prev = gated_norm(it, o_heads)
    out_project(n_items - 1, y_prev)


def _hgrn2_layer(x, mod, gain, w, lb_raw, gn_gain, w_out, final_gain, layer):
    bsz, seq, d = x.shape
    tm = HGRN2_TILES_PER_STEP * TILE_ROWS
    const = lambda nd: (lambda b, t: (0,) * nd)
    resident = functools.partial(pl.BlockSpec, pipeline_mode=pl.Buffered(1))
    in_hbm = pl.BlockSpec(memory_space=pl.ANY)
    return pl.pallas_call(
        functools.partial(_hgrn2_kernel, layer=layer),
        grid=(bsz, seq // tm),
        in_specs=[
            pl.BlockSpec((1, tm, d), lambda b, t: (b, t, 0)),
            pl.BlockSpec((1, 3, d), lambda b, t: (b, 0, 0)),
            resident((1, d), const(2)),
            in_hbm,
            resident(lb_raw.shape, const(3)),
            resident((1, HG_HEAD_DIM), const(2)),
            in_hbm,
            resident((1, d), const(2)),
        ],
        out_specs=pl.BlockSpec((1, tm, d), lambda b, t: (b, t, 0)),
        out_shape=jax.ShapeDtypeStruct(x.shape, x.dtype),
        scratch_shapes=[
            pltpu.VMEM((HGRN2_TILES_PER_STEP, TILE_ROWS, d), _BF16),
            pltpu.VMEM((HGRN2_TILES_PER_STEP, TILE_ROWS, d), _F32),
            pltpu.VMEM((HG_HEADS, HG_HEAD_DIM, HG_HEAD_DIM), _F32),
            pltpu.VMEM((PROJ_SLOTS, 4, TILE_ROWS, PAIR_DIM), _F32),
            pltpu.VMEM((HEAD_PAIRS, 4, d, PAIR_DIM), _BF16),
            pltpu.VMEM((HEAD_PAIRS, d // PAIR_DIM, PAIR_DIM, PAIR_DIM), _BF16),
            pltpu.VMEM((STAGE_SLOTS, d, PAIR_DIM), _F32),
            pltpu.VMEM((STAGE_SLOTS, PAIR_DIM, PAIR_DIM), _F32),
            pltpu.SemaphoreType.DMA((STAGE_SLOTS,)),
            pltpu.SemaphoreType.DMA((STAGE_SLOTS,)),
        ],
        compiler_params=pltpu.CompilerParams(
            dimension_semantics=("arbitrary", "arbitrary"), vmem_limit_bytes=VMEM_LIMIT_BYTES),
        name="hgrn2_layer",
    )(x, mod, gain, w, lb_raw, gn_gain, w_out, final_gain)


def kernel(x, c, norm_gain, w_ada, b_ada, a_w_in, a_ln_gain, a_ln_bias, a_w_s, a_b_s, a_w_out,
           b_w_in, b_lower_bounds, b_gn_gain, b_w_out, final_gain):
    bsz, seq, d = x.shape
    depth = norm_gain.shape[0]
    assert depth == 2 and a_w_in.shape[0] == 1 and b_w_in.shape[0] == 1
    assert seq % TOKENS_PER_STEP == 0 and TOKENS_PER_STEP % SG_BLOCK == 0
    assert seq % (HGRN2_TILES_PER_STEP * TILE_ROWS) == 0

    mod = _adaln_mod(c, w_ada, b_ada)
    mod = mod.reshape(depth, bsz, 3, d)

    x1 = _sgu_layer(
        x, mod[0], norm_gain[0:1], a_w_in, a_ln_gain, a_ln_bias, a_w_s[0],
        a_b_s[0].reshape(SG_GROUPS, SG_BLOCK, 1), a_w_out)

    lb_raw = b_lower_bounds.reshape(depth, HEAD_PAIRS, PAIR_DIM).transpose(1, 0, 2)
    return _hgrn2_layer(
        x1, mod[1], norm_gain[1:2], b_w_in, lb_raw, b_gn_gain, b_w_out, final_gain.reshape(1, d),
        layer=1)
```

```python
import functools
import math

import jax
import jax.numpy as jnp
from jax import lax
from jax.experimental import pallas as pl
from jax.experimental.pallas import tpu as pltpu

D_MODEL = 1024
D_INNER = 2048
CHUNK = 64
SG_BLOCK = 128
SG_GROUPS = 8
SG_GROUP_DIM = D_INNER // SG_GROUPS
HG_HEADS = 16
HG_HEAD_DIM = D_INNER // HG_HEADS
HEAD_PAIRS = HG_HEADS // 2
PAIR_DIM = 2 * HG_HEAD_DIM
EPS = 1e-6

TOKENS_PER_STEP = 256
TILE_ROWS = 256
HGRN2_TILES_PER_STEP = 1
PROJ_SLOTS = 2
STAGE_SLOTS = 4
VMEM_LIMIT_BYTES = 56 * 1024 * 1024

_BF16 = jnp.bfloat16
_F32 = jnp.float32


def _dot(a, b):
    return jnp.dot(a, b, preferred_element_type=_F32)


def _sigmoid(z):
    return 1.0 / (1.0 + jnp.exp(-z))


def _silu(z):
    return z * _sigmoid(z)


def _gelu_tanh(z):
    c = math.sqrt(2.0 / math.pi)
    return 0.5 * z * (1.0 + jnp.tanh(c * (z + 0.044715 * (z * z * z))))


def _modulated_norm(x, gain, mod_ref):
    r = lax.rsqrt(jnp.mean(x * x, axis=-1, keepdims=True) + EPS)
    shift = mod_ref[0, 0:1, :]
    scale = mod_ref[0, 1:2, :]
    return (x * r) * (gain * (1.0 + scale)) + shift


def _stage_weight_blocks(src_hbm, dst_ref, stage_ref, sem_ref, blocks):
    slots = stage_ref.shape[0]

    def block_copy(j):
        return pltpu.make_async_copy(
            src_hbm.at[blocks[j][0]], stage_ref.at[j % slots], sem_ref.at[j % slots])

    for j in range(min(slots - 1, len(blocks))):
        block_copy(j).start()
    for j in range(len(blocks)):
        if j + slots - 1 < len(blocks):
            block_copy(j + slots - 1).start()
        block_copy(j).wait()
        dst_ref[blocks[j][1]] = stage_ref[j % slots].astype(_BF16)


def _first_grid_step():
    return (pl.program_id(0) == 0) & (pl.program_id(1) == 0)


def _adaln_kernel(c_ref, w_ref, b_ref, o_ref):
    c_act = _silu(c_ref[...]).astype(_BF16)
    o_ref[0] = _dot(c_act, w_ref[0].astype(_BF16)) + b_ref[0]


def _adaln_mod(c, w_ada, b_ada):
    depth, d, n3 = w_ada.shape
    bsz = c.shape[0]
    nt = n3 // d
    return pl.pallas_call(
        _adaln_kernel,
        grid=(depth, nt),
        in_specs=[
            pl.BlockSpec((bsz, d), lambda l, j: (0, 0)),
            pl.BlockSpec((1, d, d), lambda l, j: (l, 0, j)),
            pl.BlockSpec((1, 1, d), lambda l, j: (l, 0, j)),
        ],
        out_specs=pl.BlockSpec((1, bsz, d), lambda l, j: (l, 0, j)),
        out_shape=jax.ShapeDtypeStruct((depth, bsz, n3), _F32),
        compiler_params=pltpu.CompilerParams(dimension_semantics=("arbitrary", "arbitrary")),
        name="adaln_mod",
    )(c, w_ada, b_ada.reshape(depth, 1, n3))


def _sgu_kernel(x_ref, mod_ref, xnext_ref, modnext_ref, gain_ref, win_hbm, lng_ref, lnb_ref, ws_ref, bs_ref, wout_hbm,
                o_ref, h_ref, v_ref, ug_ref, acc_ref, win_ref, wout_ref, win_stage, wout_stage, win_sems, wout_sems):
    tm = x_ref.shape[1]
    U, V, GATE = range(3)

    @pl.when(_first_grid_step())
    def _():
        _stage_weight_blocks(
            win_hbm, win_ref, win_stage, win_sems,
            [((0, pl.ds(0, D_MODEL), pl.ds(p * D_INNER + g * SG_GROUP_DIM, SG_GROUP_DIM)), (p, g))
             for p in range(3) for g in range(SG_GROUPS)])
        _stage_weight_blocks(
            wout_hbm, wout_ref, wout_stage, wout_sems,
            [((0, pl.ds(g * SG_GROUP_DIM, SG_GROUP_DIM), pl.ds(0, D_MODEL)), (g,)) for g in range(SG_GROUPS)])

        h_ref[0] = _modulated_norm(x_ref[0], gain_ref[...], mod_ref).astype(_BF16)

    def step_body(cur, nxt):
        t_chunk = lax.broadcasted_iota(jnp.int32, (SG_BLOCK, SG_BLOCK), 0) // CHUNK
        s_chunk = lax.broadcasted_iota(jnp.int32, (SG_BLOCK, SG_BLOCK), 1) // CHUNK
        allowed = s_chunk <= t_chunk

        def group_cols(g):
            return slice(g * SG_GROUP_DIM, (g + 1) * SG_GROUP_DIM)

        def project_ug(g):
            ug_ref[g % 2, 0] = _dot(h_ref[cur], win_ref[U, g])
            ug_ref[g % 2, 1] = _dot(h_ref[cur], win_ref[GATE, g])

        def lane_halves_sum(z):
            return z[:, :SG_GROUP_DIM // 2] + z[:, SG_GROUP_DIM // 2:]

        row_sum = None
        for g in range(SG_GROUPS):
            vg = _gelu_tanh(_dot(h_ref[cur], win_ref[V, g]))
            v_ref[g] = vg
            row_sum = lane_halves_sum(vg) if row_sum is None else row_sum + lane_halves_sum(vg)
        mu = jnp.sum(row_sum, axis=-1, keepdims=True) * (1.0 / D_INNER)
        project_ug(0)
        h_ref[nxt] = _modulated_norm(xnext_ref[0], gain_ref[...], modnext_ref).astype(_BF16)
        sq_sum = None
        for g in range(SG_GROUPS):
            vc = v_ref[g] - mu
            sq_sum = lane_halves_sum(vc * vc) if sq_sum is None else sq_sum + lane_halves_sum(vc * vc)
        var = jnp.sum(sq_sum, axis=-1, keepdims=True) * (1.0 / D_INNER)
        rstd = lax.rsqrt(var + EPS)

        for g in range(SG_GROUPS):
            vn = ((v_ref[g] - mu) * rstd * lng_ref[:, group_cols(g)] + lnb_ref[:, group_cols(g)]).astype(_BF16)
            ws = jnp.where(allowed, ws_ref[g], 0.0).astype(_BF16)
            bias = bs_ref[g]
            s = jnp.concatenate(
                [_dot(ws, vn[blk * SG_BLOCK:(blk + 1) * SG_BLOCK, :]) + bias for blk in range(tm // SG_BLOCK)],
                axis=0)
            if g + 1 < SG_GROUPS:
                project_ug(g + 1)
            u = _gelu_tanh(ug_ref[g % 2, 0])
            gate = _silu(ug_ref[g % 2, 1])
            y = (u * s * gate).astype(_BF16)
            contrib = _dot(y, wout_ref[g])
            if g == 0:
                acc_ref[...] = contrib
            else:
                acc_ref[...] += contrib

        o_ref[0] = x_ref[0] + mod_ref[0, 2:3, :] * acc_ref[...]

    step = pl.program_id(0) * pl.num_programs(1) + pl.program_id(1)
    for parity in range(2):
        pl.when(step % 2 == parity)(functools.partial(step_body, parity, 1 - parity))


def _sgu_layer(x, mod, gain, w_in, ln_gain, ln_bias, w_s, b_s, w_out):
    bsz, seq, d = x.shape
    tm = TOKENS_PER_STEP
    const = lambda nd: (lambda b, t: (0,) * nd)
    resident = functools.partial(pl.BlockSpec, pipeline_mode=pl.Buffered(1))
    in_hbm = pl.BlockSpec(memory_space=pl.ANY)
    n_t = seq // tm

    def next_step(b, t):
        nxt = jnp.minimum(b * n_t + t + 1, bsz * n_t - 1)
        return nxt // n_t, nxt % n_t

    return pl.pallas_call(
        _sgu_kernel,
        grid=(bsz, n_t),
        in_specs=[
            pl.BlockSpec((1, tm, d), lambda b, t: (b, t, 0)),
            pl.BlockSpec((1, 3, d), lambda b, t: (b, 0, 0)),
            pl.BlockSpec((1, tm, d), lambda b, t: (*next_step(b, t), 0)),
            pl.BlockSpec((1, 3, d), lambda b, t: (next_step(b, t)[0], 0, 0)),
            resident((1, d), const(2)),
            in_hbm,
            resident((1, D_INNER), const(2)),
            resident((1, D_INNER), const(2)),
            resident(w_s.shape, const(3)),
            resident(b_s.shape, const(3)),
            in_hbm,
        ],
        out_specs=pl.BlockSpec((1, tm, d), lambda b, t: (b, t, 0)),
        out_shape=jax.ShapeDtypeStruct(x.shape, x.dtype),
        scratch_shapes=[
            pltpu.VMEM((2, tm, d), _BF16),
            pltpu.VMEM((SG_GROUPS, tm, SG_GROUP_DIM), _F32),
            pltpu.VMEM((2, 2, tm, SG_GROUP_DIM), _F32),
            pltpu.VMEM((tm, d), _F32),
            pltpu.VMEM((3, SG_GROUPS, d, SG_GROUP_DIM), _BF16),
            pltpu.VMEM((SG_GROUPS, SG_GROUP_DIM, d), _BF16),
            pltpu.VMEM((STAGE_SLOTS, d, SG_GROUP_DIM), _F32),
            pltpu.VMEM((STAGE_SLOTS, SG_GROUP_DIM, d), _F32),
            pltpu.SemaphoreType.DMA((STAGE_SLOTS,)),
            pltpu.SemaphoreType.DMA((STAGE_SLOTS,)),
        ],
        compiler_params=pltpu.CompilerParams(
            dimension_semantics=("arbitrary", "arbitrary"), vmem_limit_bytes=VMEM_LIMIT_BYTES),
        name="sgu_layer",
    )(x, mod, x, mod, gain, w_in, ln_gain, ln_bias, w_s, b_s, w_out)


def _hgrn2_kernel(x_ref, mod_ref, gain_ref, w_hbm, lbraw_ref, gn_ref, wout_hbm, fgain_ref,
                  o_ref, h_ref, acc_ref, state_ref, proj_ref, w_ref, wout_ref, w_stage, wout_stage, w_sems, wout_sems,
                  *, layer):
    tm = TILE_ROWS
    n_tiles = x_ref.shape[1] // tm
    n_items = n_tiles * HEAD_PAIRS
    n_chunks = tm // CHUNK
    Q, F, I, G = range(4)

    @pl.when(_first_grid_step())
    def _():
        _stage_weight_blocks(
            w_hbm, w_ref, w_stage, w_sems,
            [((0, pl.ds(0, D_MODEL), pl.ds(part * D_INNER + hp * PAIR_DIM, PAIR_DIM)), (hp, part))
             for part in range(4) for hp in range(HEAD_PAIRS)])
        _stage_weight_blocks(
            wout_hbm, wout_ref, wout_stage, wout_sems,
            [((0, pl.ds(hp * PAIR_DIM, PAIR_DIM), pl.ds(n * PAIR_DIM, PAIR_DIM)), (hp, n))
             for hp in range(HEAD_PAIRS) for n in range(D_MODEL // PAIR_DIM)])

    @pl.when(pl.program_id(1) == 0)
    def _():
        state_ref[...] = jnp.zeros_like(state_ref)

    row = lax.broadcasted_iota(jnp.int32, (tm, tm), 0)
    col = lax.broadcasted_iota(jnp.int32, (tm, tm), 1)
    intra = (row // CHUNK == col // CHUNK) & (col <= row)
    step_in_chunk = lax.broadcasted_iota(jnp.int32, (tm, PAIR_DIM), 0) % CHUNK
    gn = gn_ref[...]
    heads = [slice(hd * HG_HEAD_DIM, (hd + 1) * HG_HEAD_DIM) for hd in range(2)]
    chunks = [slice(c * CHUNK, (c + 1) * CHUNK) for c in range(n_chunks)]
    row_halves = [slice(0, tm // 2), slice(tm // 2, tm)]

    def tile_rows(tile):
        return slice(tile * tm, (tile + 1) * tm)

    def normalize_tile(tile):
        h_ref[tile] = _modulated_norm(x_ref[0, tile_rows(tile)], gain_ref[...], mod_ref).astype(_BF16)

    def finish_tile(tile):
        xo = x_ref[0, tile_rows(tile)] + mod_ref[0, 2:3, :] * acc_ref[tile]
        r = lax.rsqrt(jnp.mean(xo * xo, axis=-1, keepdims=True) + EPS)
        o_ref[0, tile_rows(tile)] = xo * r * fgain_ref[...]

    def project(it, part):
        tile, hp = divmod(it, HEAD_PAIRS)
        for rows in row_halves:
            proj_ref[it % PROJ_SLOTS, part, rows] = _dot(h_ref[tile, rows], w_ref[hp, part])

    def forget_gate(it):
        hp = it % HEAD_PAIRS
        lbraw = lbraw_ref[hp]
        e = jnp.exp(lbraw - jnp.max(lbraw, axis=0, keepdims=True))
        p = e / jnp.sum(e, axis=0, keepdims=True)
        lb = jnp.sum(p[1:layer + 1], axis=0, keepdims=True)
        f = lb + (1.0 - lb) * _sigmoid(proj_ref[it % PROJ_SLOTS, F])
        return 1.0 - f, jnp.log(f)

    def prefix_sum(z):
        shift = 1
        while shift < CHUNK:
            z = z + jnp.where(step_in_chunk >= shift, pltpu.roll(z, shift, axis=0), 0.0)
            shift *= 2
        return z

    def decay_scaling(it, k, a):
        def chunk_rows(offset):
            return jnp.concatenate(
                [jnp.broadcast_to(a[c * CHUNK + offset:c * CHUNK + offset + 1, :], (CHUNK, PAIR_DIM))
                 for c in range(n_chunks)], axis=0)
        a_mid = chunk_rows(CHUNK // 2 - 1)
        a_end = chunk_rows(CHUNK - 1)
        q = _silu(proj_ref[it % PROJ_SLOTS, Q])
        q_in = q * jnp.exp(a - a_mid)
        k_in = k * jnp.exp(a_mid - a)
        q_out = (q_in * jnp.exp(a_mid)).astype(_BF16)
        k_out = (k_in * jnp.exp(a_end - a_mid)).astype(_BF16)
        decay = [jnp.exp(a[c * CHUNK + CHUNK - 1:c * CHUNK + CHUNK, :]) for c in range(n_chunks)]
        k_in_t = k_in.T.astype(_BF16)
        return q_in.astype(_BF16), k_in_t, q_out, k_out, decay

    def scores_and_updates(it, q_in, k_in_t, k_out):
        vv32 = proj_ref[it % PROJ_SLOTS, I]
        vv = vv32.astype(_BF16)
        vv_t = vv32.T.astype(_BF16)
        scores = [_dot(q_in[:, ln], k_in_t[ln, :]) for ln in heads]
        kv = []
        for ln in heads:
            blocks = []
            for c, rw in enumerate(chunks):
                pieces = [k_out[rw, ln]]
                if c > 0:
                    pieces.insert(0, jnp.zeros((c * CHUNK, HG_HEAD_DIM), _BF16))
                if c + 1 < n_chunks:
                    pieces.append(jnp.zeros(((n_chunks - 1 - c) * CHUNK, HG_HEAD_DIM), _BF16))
                blocks.append(jnp.concatenate(pieces, axis=0))
            kv.append(_dot(vv_t[ln, :], jnp.concatenate(blocks, axis=1)))
        return vv, scores, kv

    def mask_and_states(it, scores, kv, decay):
        hp = it % HEAD_PAIRS
        scores = [jnp.where(intra, s, 0.0).astype(_BF16) for s in scores]
        states = []
        for hd, ln in enumerate(heads):
            st = state_ref[hp * 2 + hd]
            per_chunk = []
            for c in range(n_chunks):
                per_chunk.append(st.T.astype(_BF16))
                st = decay[c][:, ln] * st + kv[hd][:, c * HG_HEAD_DIM:(c + 1) * HG_HEAD_DIM]
            state_ref[hp * 2 + hd] = st
            states.append(per_chunk)
        return scores, states

    def outputs(vv, scores, q_out, states):
        o_heads = []
        for hd, ln in enumerate(heads):
            o_chunks = []
            for c, rw in enumerate(chunks):
                blk = slice((c // 2) * SG_BLOCK, (c // 2 + 1) * SG_BLOCK)
                lhs = jnp.concatenate([q_out[rw, ln], scores[hd][rw, blk]], axis=1)
                rhs = jnp.concatenate([states[hd][c], vv[blk, ln]], axis=0)
                o_chunks.append(_dot(lhs, rhs))
            o_heads.append(jnp.concatenate(o_chunks, axis=0))
        return o_heads

    def gated_norm(it, o_heads):
        og = _silu(proj_ref[it % PROJ_SLOTS, G])
        normed = [o * lax.rsqrt(jnp.mean(o * o, axis=-1, keepdims=True) + EPS) * gn for o in o_heads]
        return (jnp.concatenate(normed, axis=1) * og).astype(_BF16)

    def out_project(it, y):
        tile, hp = divmod(it, HEAD_PAIRS)
        for n in range(D_MODEL // PAIR_DIM):
            cols = slice(n * PAIR_DIM, (n + 1) * PAIR_DIM)
            contrib = _dot(y, wout_ref[hp, n])
            if hp == 0:
                acc_ref[tile, :, cols] = contrib
            else:
                acc_ref[tile, :, cols] += contrib
        if hp == HEAD_PAIRS - 1:
            finish_tile(tile)

    normalize_tile(0)
    for part in (F, Q, I, G):
        project(0, part)
    y_prev = None
    for it in range(n_items):
        tile, hp = divmod(it, HEAD_PAIRS)
        nxt = it + 1 if it + 1 < n_items else None
        if hp == HEAD_PAIRS // 2 and tile + 1 < n_tiles:
            normalize_tile(tile + 1)
        k, log_f = forget_gate(it)
        a = prefix_sum(log_f)
        if y_prev is not None:
            out_project(it - 1, y_prev)
        if nxt is not None:
            project(nxt, F)
        q_in, k_in_t, q_out, k_out, decay = decay_scaling(it, k, a)
        vv, scores, kv = scores_and_updates(it, q_in, k_in_t, k_out)
        if nxt is not None:
            project(nxt, Q)
        scores, states = mask_and_states(it, scores, kv, decay)
        o_heads = outputs(vv, scores, q_out, states)
        if nxt is not None:
            project(nxt, I)
            project(nxt, G)
        y_prev = gated_norm(it, o_heads)
    out_project(n_items - 1, y_prev)


def _hgrn2_layer(x, mod, gain, w, lb_raw, gn_gain, w_out, final_gain, layer):
    bsz, seq, d = x.shape
    tm = HGRN2_TILES_PER_STEP * TILE_ROWS
    const = lambda nd: (lambda b, t: (0,) * nd)
    resident = functools.partial(pl.BlockSpec, pipeline_mode=pl.Buffered(1))
    in_hbm = pl.BlockSpec(memory_space=pl.ANY)
    return pl.pallas_call(
        functools.partial(_hgrn2_kernel, layer=layer),
        grid=(bsz, seq // tm),
        in_specs=[
            pl.BlockSpec((1, tm, d), lambda b, t: (b, t, 0)),
            pl.BlockSpec((1, 3, d), lambda b, t: (b, 0, 0)),
            resident((1, d), const(2)),
            in_hbm,
            resident(lb_raw.shape, const(3)),
            resident((1, HG_HEAD_DIM), const(2)),
            in_hbm,
            resident((1, d), const(2)),
        ],
        out_specs=pl.BlockSpec((1, tm, d), lambda b, t: (b, t, 0)),
        out_shape=jax.ShapeDtypeStruct(x.shape, x.dtype),
        scratch_shapes=[
            pltpu.VMEM((HGRN2_TILES_PER_STEP, TILE_ROWS, d), _BF16),
            pltpu.VMEM((HGRN2_TILES_PER_STEP, TILE_ROWS, d), _F32),
            pltpu.VMEM((HG_HEADS, HG_HEAD_DIM, HG_HEAD_DIM), _F32),
            pltpu.VMEM((PROJ_SLOTS, 4, TILE_ROWS, PAIR_DIM), _F32),
            pltpu.VMEM((HEAD_PAIRS, 4, d, PAIR_DIM), _BF16),
            pltpu.VMEM((HEAD_PAIRS, d // PAIR_DIM, PAIR_DIM, PAIR_DIM), _BF16),
            pltpu.VMEM((STAGE_SLOTS, d, PAIR_DIM), _F32),
            pltpu.VMEM((STAGE_SLOTS, PAIR_DIM, PAIR_DIM), _F32),
            pltpu.SemaphoreType.DMA((STAGE_SLOTS,)),
            pltpu.SemaphoreType.DMA((STAGE_SLOTS,)),
        ],
        compiler_params=pltpu.CompilerParams(
            dimension_semantics=("arbitrary", "arbitrary"), vmem_limit_bytes=VMEM_LIMIT_BYTES),
        name="hgrn2_layer",
    )(x, mod, gain, w, lb_raw, gn_gain, w_out, final_gain)


def kernel(x, c, norm_gain, w_ada, b_ada, a_w_in, a_ln_gain, a_ln_bias, a_w_s, a_b_s, a_w_out,
           b_w_in, b_lower_bounds, b_gn_gain, b_w_out, final_gain):
    bsz, seq, d = x.shape
    depth = norm_gain.shape[0]
    assert depth == 2 and a_w_in.shape[0] == 1 and b_w_in.shape[0] == 1
    assert seq % TOKENS_PER_STEP == 0 and TOKENS_PER_STEP % SG_BLOCK == 0
    assert seq % (HGRN2_TILES_PER_STEP * TILE_ROWS) == 0

    mod = _adaln_mod(c, w_ada, b_ada)
    mod = mod.reshape(depth, bsz, 3, d)

    x1 = _sgu_layer(
        x, mod[0], norm_gain[0:1], a_w_in, a_ln_gain, a_ln_bias, a_w_s[0],
        a_b_s[0].reshape(SG_GROUPS, SG_BLOCK, 1), a_w_out)

    lb_raw = b_lower_bounds.reshape(depth, HEAD_PAIRS, PAIR_DIM).transpose(1, 0, 2)
    return _hgrn2_layer(
        x1, mod[1], norm_gain[1:2], b_w_in, lb_raw, b_gn_gain, b_w_out, final_gain.reshape(1, d),
        layer=1)
```

```python
import functools
import math

import jax
import jax.numpy as jnp
from jax import lax
from jax.experimental import pallas as pl
from jax.experimental.pallas import tpu as pltpu

D_MODEL = 1024
D_INNER = 2048
CHUNK = 64
SG_BLOCK = 128
SG_GROUPS = 8
SG_GROUP_DIM = D_INNER // SG_GROUPS
HG_HEADS = 16
HG_HEAD_DIM = D_INNER // HG_HEADS
HEAD_PAIRS = HG_HEADS // 2
PAIR_DIM = 2 * HG_HEAD_DIM
EPS = 1e-6

TOKENS_PER_STEP = 256
TILE_ROWS = 256
HGRN2_TILES_PER_STEP = 1
PROJ_SLOTS = 2
STAGE_SLOTS = 4
VMEM_LIMIT_BYTES = 56 * 1024 * 1024

_BF16 = jnp.bfloat16
_F32 = jnp.float32


def _dot(a, b):
    return jnp.dot(a, b, preferred_element_type=_F32)


def _sigmoid(z):
    return 1.0 / (1.0 + jnp.exp(-z))


def _silu(z):
    return z * _sigmoid(z)


def _gelu_tanh(z):
    c = math.sqrt(2.0 / math.pi)
    return 0.5 * z * (1.0 + jnp.tanh(c * (z + 0.044715 * (z * z * z))))


def _modulated_norm(x, gain, mod_ref):
    r = lax.rsqrt(jnp.mean(x * x, axis=-1, keepdims=True) + EPS)
    shift = mod_ref[0, 0:1, :]
    scale = mod_ref[0, 1:2, :]
    return (x * r) * (gain * (1.0 + scale)) + shift


def _stage_weight_blocks(src_hbm, dst_ref, stage_ref, sem_ref, blocks):
    slots = stage_ref.shape[0]

    def block_copy(j):
        return pltpu.make_async_copy(
            src_hbm.at[blocks[j][0]], stage_ref.at[j % slots], sem_ref.at[j % slots])

    for j in range(min(slots - 1, len(blocks))):
        block_copy(j).start()
    for j in range(len(blocks)):
        if j + slots - 1 < len(blocks):
            block_copy(j + slots - 1).start()
        block_copy(j).wait()
        dst_ref[blocks[j][1]] = stage_ref[j % slots].astype(_BF16)


def _first_grid_step():
    return (pl.program_id(0) == 0) & (pl.program_id(1) == 0)


def _adaln_kernel(c_ref, w_ref, b_ref, o_ref):
    c_act = _silu(c_ref[...]).astype(_BF16)
    o_ref[0] = _dot(c_act, w_ref[0].astype(_BF16)) + b_ref[0]


def _adaln_mod(c, w_ada, b_ada):
    depth, d, n3 = w_ada.shape
    bsz = c.shape[0]
    nt = n3 // d
    return pl.pallas_call(
        _adaln_kernel,
        grid=(depth, nt),
        in_specs=[
            pl.BlockSpec((bsz, d), lambda l, j: (0, 0)),
            pl.BlockSpec((1, d, d), lambda l, j: (l, 0, j)),
            pl.BlockSpec((1, 1, d), lambda l, j: (l, 0, j)),
        ],
        out_specs=pl.BlockSpec((1, bsz, d), lambda l, j: (l, 0, j)),
        out_shape=jax.ShapeDtypeStruct((depth, bsz, n3), _F32),
        compiler_params=pltpu.CompilerParams(dimension_semantics=("arbitrary", "arbitrary")),
        name="adaln_mod",
    )(c, w_ada, b_ada.reshape(depth, 1, n3))


def _sgu_kernel(x_ref, mod_ref, gain_ref, win_hbm, lng_ref, lnb_ref, ws_ref, bs_ref, wout_hbm,
                o_ref, h_ref, v_ref, ug_ref, acc_ref, win_ref, wout_ref, win_stage, wout_stage, win_sems, wout_sems):
    tm = x_ref.shape[1]
    U, V, GATE = range(3)

    @pl.when(_first_grid_step())
    def _():
        _stage_weight_blocks(
            win_hbm, win_ref, win_stage, win_sems,
            [((0, pl.ds(0, D_MODEL), pl.ds(p * D_INNER + g * SG_GROUP_DIM, SG_GROUP_DIM)), (p, g))
             for p in range(3) for g in range(SG_GROUPS)])
        _stage_weight_blocks(
            wout_hbm, wout_ref, wout_stage, wout_sems,
            [((0, pl.ds(g * SG_GROUP_DIM, SG_GROUP_DIM), pl.ds(0, D_MODEL)), (g,)) for g in range(SG_GROUPS)])

    x = x_ref[0]
    h_ref[...] = _modulated_norm(x, gain_ref[...], mod_ref).astype(_BF16)

    t_chunk = lax.broadcasted_iota(jnp.int32, (SG_BLOCK, SG_BLOCK), 0) // CHUNK
    s_chunk = lax.broadcasted_iota(jnp.int32, (SG_BLOCK, SG_BLOCK), 1) // CHUNK
    allowed = s_chunk <= t_chunk

    def group_cols(g):
        return slice(g * SG_GROUP_DIM, (g + 1) * SG_GROUP_DIM)

    def project_ug(g):
        ug_ref[g % 2, 0] = _dot(h_ref[...], win_ref[U, g])
        ug_ref[g % 2, 1] = _dot(h_ref[...], win_ref[GATE, g])

    def lane_halves_sum(z):
        return z[:, :SG_GROUP_DIM // 2] + z[:, SG_GROUP_DIM // 2:]

    def out_project(g, y):
        contrib = _dot(y, wout_ref[g])
        if g == 0:
            acc_ref[...] = contrib
        else:
            acc_ref[...] += contrib

    row_sum = None
    for g in range(SG_GROUPS):
        vg = _gelu_tanh(_dot(h_ref[...], win_ref[V, g]))
        v_ref[g] = vg
        row_sum = lane_halves_sum(vg) if row_sum is None else row_sum + lane_halves_sum(vg)
    mu = jnp.sum(row_sum, axis=-1, keepdims=True) * (1.0 / D_INNER)
    project_ug(0)
    sq_sum = None
    for g in range(SG_GROUPS):
        vc = v_ref[g] - mu
        sq_sum = lane_halves_sum(vc * vc) if sq_sum is None else sq_sum + lane_halves_sum(vc * vc)
    var = jnp.sum(sq_sum, axis=-1, keepdims=True) * (1.0 / D_INNER)
    rstd = lax.rsqrt(var + EPS)

    for g in range(SG_GROUPS):
        vn = ((v_ref[g] - mu) * rstd * lng_ref[:, group_cols(g)] + lnb_ref[:, group_cols(g)]).astype(_BF16)
        ws = jnp.where(allowed, ws_ref[g], 0.0).astype(_BF16)
        bias = bs_ref[g]
        s = jnp.concatenate(
            [_dot(ws, vn[blk * SG_BLOCK:(blk + 1) * SG_BLOCK, :]) + bias for blk in range(tm // SG_BLOCK)],
            axis=0)
        if g + 1 < SG_GROUPS:
            project_ug(g + 1)
        if g > 0:
            out_project(g - 1, y_prev)
        u = _gelu_tanh(ug_ref[g % 2, 0])
        gate = _silu(ug_ref[g % 2, 1])
        y_prev = (u * s * gate).astype(_BF16)
    out_project(SG_GROUPS - 1, y_prev)

    o_ref[0] = x + mod_ref[0, 2:3, :] * acc_ref[...]


def _sgu_layer(x, mod, gain, w_in, ln_gain, ln_bias, w_s, b_s, w_out):
    bsz, seq, d = x.shape
    tm = TOKENS_PER_STEP
    const = lambda nd: (lambda b, t: (0,) * nd)
    resident = functools.partial(pl.BlockSpec, pipeline_mode=pl.Buffered(1))
    in_hbm = pl.BlockSpec(memory_space=pl.ANY)
    return pl.pallas_call(
        _sgu_kernel,
        grid=(bsz, seq // tm),
        in_specs=[
            pl.BlockSpec((1, tm, d), lambda b, t: (b, t, 0)),
            pl.BlockSpec((1, 3, d), lambda b, t: (b, 0, 0)),
            resident((1, d), const(2)),
            in_hbm,
            resident((1, D_INNER), const(2)),
            resident((1, D_INNER), const(2)),
            resident(w_s.shape, const(3)),
            resident(b_s.shape, const(3)),
            in_hbm,
        ],
        out_specs=pl.BlockSpec((1, tm, d), lambda b, t: (b, t, 0)),
        out_shape=jax.ShapeDtypeStruct(x.shape, x.dtype),
        scratch_shapes=[
            pltpu.VMEM((tm, d), _BF16),
            pltpu.VMEM((SG_GROUPS, tm, SG_GROUP_DIM), _F32),
            pltpu.VMEM((2, 2, tm, SG_GROUP_DIM), _F32),
            pltpu.VMEM((tm, d), _F32),
            pltpu.VMEM((3, SG_GROUPS, d, SG_GROUP_DIM), _BF16),
            pltpu.VMEM((SG_GROUPS, SG_GROUP_DIM, d), _BF16),
            pltpu.VMEM((STAGE_SLOTS, d, SG_GROUP_DIM), _F32),
            pltpu.VMEM((STAGE_SLOTS, SG_GROUP_DIM, d), _F32),
            pltpu.SemaphoreType.DMA((STAGE_SLOTS,)),
            pltpu.SemaphoreType.DMA((STAGE_SLOTS,)),
        ],
        compiler_params=pltpu.CompilerParams(
            dimension_semantics=("arbitrary", "arbitrary"), vmem_limit_bytes=VMEM_LIMIT_BYTES),
        name="sgu_layer",
    )(x, mod, gain, w_in, ln_gain, ln_bias, w_s, b_s, w_out)


def _hgrn2_kernel(x_ref, mod_ref, gain_ref, w_hbm, lbraw_ref, gn_ref, wout_hbm, fgain_ref,
                  o_ref, h_ref, acc_ref, state_ref, proj_ref, w_ref, wout_ref, w_stage, wout_stage, w_sems, wout_sems,
                  *, layer):
    tm = TILE_ROWS
    n_tiles = x_ref.shape[1] // tm
    n_items = n_tiles * HEAD_PAIRS
    n_chunks = tm // CHUNK
    Q, F, I, G = range(4)

    @pl.when(_first_grid_step())
    def _():
        _stage_weight_blocks(
            w_hbm, w_ref, w_stage, w_sems,
            [((0, pl.ds(0, D_MODEL), pl.ds(part * D_INNER + hp * PAIR_DIM, PAIR_DIM)), (hp, part))
             for part in range(4) for hp in range(HEAD_PAIRS)])
        _stage_weight_blocks(
            wout_hbm, wout_ref, wout_stage, wout_sems,
            [((0, pl.ds(hp * PAIR_DIM, PAIR_DIM), pl.ds(n * PAIR_DIM, PAIR_DIM)), (hp, n))
             for hp in range(HEAD_PAIRS) for n in range(D_MODEL // PAIR_DIM)])

    @pl.when(pl.program_id(1) == 0)
    def _():
        state_ref[...] = jnp.zeros_like(state_ref)

    row = lax.broadcasted_iota(jnp.int32, (tm, tm), 0)
    col = lax.broadcasted_iota(jnp.int32, (tm, tm), 1)
    intra = (row // CHUNK == col // CHUNK) & (col <= row)
    step_in_chunk = lax.broadcasted_iota(jnp.int32, (tm, PAIR_DIM), 0) % CHUNK
    gn = gn_ref[...]
    heads = [slice(hd * HG_HEAD_DIM, (hd + 1) * HG_HEAD_DIM) for hd in range(2)]
    chunks = [slice(c * CHUNK, (c + 1) * CHUNK) for c in range(n_chunks)]
    row_halves = [slice(0, tm // 2), slice(tm // 2, tm)]

    def tile_rows(tile):
        return slice(tile * tm, (tile + 1) * tm)

    def normalize_tile(tile):
        h_ref[tile] = _modulated_norm(x_ref[0, tile_rows(tile)], gain_ref[...], mod_ref).astype(_BF16)

    def finish_tile(tile):
        xo = x_ref[0, tile_rows(tile)] + mod_ref[0, 2:3, :] * acc_ref[tile]
        r = lax.rsqrt(jnp.mean(xo * xo, axis=-1, keepdims=True) + EPS)
        o_ref[0, tile_rows(tile)] = xo * r * fgain_ref[...]

    def project(it, part):
        tile, hp = divmod(it, HEAD_PAIRS)
        for rows in row_halves:
            proj_ref[it % PROJ_SLOTS, part, rows] = _dot(h_ref[tile, rows], w_ref[hp, part])

    def forget_gate(it):
        hp = it % HEAD_PAIRS
        lbraw = lbraw_ref[hp]
        e = jnp.exp(lbraw - jnp.max(lbraw, axis=0, keepdims=True))
        p = e / jnp.sum(e, axis=0, keepdims=True)
        lb = jnp.sum(p[1:layer + 1], axis=0, keepdims=True)
        f = lb + (1.0 - lb) * _sigmoid(proj_ref[it % PROJ_SLOTS, F])
        return 1.0 - f, jnp.log(f)

    def prefix_sum(z):
        shift = 1
        while shift < CHUNK:
            z = z + jnp.where(step_in_chunk >= shift, pltpu.roll(z, shift, axis=0), 0.0)
            shift *= 2
        return z

    def decay_scaling(it, k, a):
        def chunk_rows(offset):
            return jnp.concatenate(
                [jnp.broadcast_to(a[c * CHUNK + offset:c * CHUNK + offset + 1, :], (CHUNK, PAIR_DIM))
                 for c in range(n_chunks)], axis=0)
        a_mid = chunk_rows(CHUNK // 2 - 1)
        a_end = chunk_rows(CHUNK - 1)
        q = _silu(proj_ref[it % PROJ_SLOTS, Q])
        q_in = q * jnp.exp(a - a_mid)
        k_in = k * jnp.exp(a_mid - a)
        q_out = (q_in * jnp.exp(a_mid)).astype(_BF16)
        k_out = (k_in * jnp.exp(a_end - a_mid)).astype(_BF16)
        decay = [jnp.exp(a[c * CHUNK + CHUNK - 1:c * CHUNK + CHUNK, :]) for c in range(n_chunks)]
        k_in_t = k_in.T.astype(_BF16)
        return q_in.astype(_BF16), k_in_t, q_out, k_out, decay

    def scores_and_updates(it, q_in, k_in_t, k_out):
        vv32 = proj_ref[it % PROJ_SLOTS, I]
        vv = vv32.astype(_BF16)
        vv_t = vv32.T.astype(_BF16)
        scores = [_dot(q_in[:, ln], k_in_t[ln, :]) for ln in heads]
        kv = []
        for ln in heads:
            blocks = []
            for c, rw in enumerate(chunks):
                pieces = [k_out[rw, ln]]
                if c > 0:
                    pieces.insert(0, jnp.zeros((c * CHUNK, HG_HEAD_DIM), _BF16))
                if c + 1 < n_chunks:
                    pieces.append(jnp.zeros(((n_chunks - 1 - c) * CHUNK, HG_HEAD_DIM), _BF16))
                blocks.append(jnp.concatenate(pieces, axis=0))
            kv.append(_dot(vv_t[ln, :], jnp.concatenate(blocks, axis=1)))
        return vv, scores, kv

    def mask_and_states(it, scores, kv, decay):
        hp = it % HEAD_PAIRS
        scores = [jnp.where(intra, s, 0.0).astype(_BF16) for s in scores]
        states = []
        for hd, ln in enumerate(heads):
            st = state_ref[hp * 2 + hd]
            per_chunk = []
            for c in range(n_chunks):
                per_chunk.append(st.T.astype(_BF16))
                st = decay[c][:, ln] * st + kv[hd][:, c * HG_HEAD_DIM:(c + 1) * HG_HEAD_DIM]
            state_ref[hp * 2 + hd] = st
            states.append(per_chunk)
        return scores, states

    def outputs(vv, scores, q_out, states):
        o_heads = []
        for hd, ln in enumerate(heads):
            o_chunks = []
            for c, rw in enumerate(chunks):
                blk = slice((c // 2) * SG_BLOCK, (c // 2 + 1) * SG_BLOCK)
                lhs = jnp.concatenate([q_out[rw, ln], scores[hd][rw, blk]], axis=1)
                rhs = jnp.concatenate([states[hd][c], vv[blk, ln]], axis=0)
                o_chunks.append(_dot(lhs, rhs))
            o_heads.append(jnp.concatenate(o_chunks, axis=0))
        return o_heads

    def gated_norm(it, o_heads):
        og = _silu(proj_ref[it % PROJ_SLOTS, G])
        normed = [o * lax.rsqrt(jnp.mean(o * o, axis=-1, keepdims=True) + EPS) * gn for o in o_heads]
        return (jnp.concatenate(normed, axis=1) * og).astype(_BF16)

    def out_project(it, y):
        tile, hp = divmod(it, HEAD_PAIRS)
        for n in range(D_MODEL // PAIR_DIM):
            cols = slice(n * PAIR_DIM, (n + 1) * PAIR_DIM)
            contrib = _dot(y, wout_ref[hp, n])
            if hp == 0:
                acc_ref[tile, :, cols] = contrib
            else:
                acc_ref[tile, :, cols] += contrib
        if hp == HEAD_PAIRS - 1:
            finish_tile(tile)

    normalize_tile(0)
    for part in (F, Q, I, G):
        project(0, part)
    y_prev = None
    for it in range(n_items):
        tile, hp = divmod(it, HEAD_PAIRS)
        nxt = it + 1 if it + 1 < n_items else None
        if hp == HEAD_PAIRS // 2 and tile + 1 < n_tiles:
            normalize_tile(tile + 1)
        k, log_f = forget_gate(it)
        a = prefix_sum(log_f)
        if y_prev is not None:
            out_project(it - 1, y_prev)
        if nxt is not None:
            project(nxt, F)
        q_in, k_in_t, q_out, k_out, decay = decay_scaling(it, k, a)
        vv, scores, kv = scores_and_updates(it, q_in, k_in_t, k_out)
        if nxt is not None:
            project(nxt, Q)
        scores, states = mask_and_states(it, scores, kv, decay)
        o_heads = outputs(vv, scores, q_out, states)
        if nxt is not None:
            project(nxt, I)
            project(nxt, G)
        y_prev = gated_norm(it, o_heads)
    out_project(n_items - 1, y_prev)


def _hgrn2_layer(x, mod, gain, w, lb_raw, gn_gain, w_out, final_gain, layer):
    bsz, seq, d = x.shape
    tm = HGRN2_TILES_PER_STEP * TILE_ROWS
    const = lambda nd: (lambda b, t: (0,) * nd)
    resident = functools.partial(pl.BlockSpec, pipeline_mode=pl.Buffered(1))
    in_hbm = pl.BlockSpec(memory_space=pl.ANY)
    return pl.pallas_call(
        functools.partial(_hgrn2_kernel, layer=layer),
        grid=(bsz, seq // tm),
        in_specs=[
            pl.BlockSpec((1, tm, d), lambda b, t: (b, t, 0)),
            pl.BlockSpec((1, 3, d), lambda b, t: (b, 0, 0)),
            resident((1, d), const(2)),
            in_hbm,
            resident(lb_raw.shape, const(3)),
            resident((1, HG_HEAD_DIM), const(2)),
            in_hbm,
            resident((1, d), const(2)),
        ],
        out_specs=pl.BlockSpec((1, tm, d), lambda b, t: (b, t, 0)),
        out_shape=jax.ShapeDtypeStruct(x.shape, x.dtype),
        scratch_shapes=[
            pltpu.VMEM((HGRN2_TILES_PER_STEP, TILE_ROWS, d), _BF16),
            pltpu.VMEM((HGRN2_TILES_PER_STEP, TILE_ROWS, d), _F32),
            pltpu.VMEM((HG_HEADS, HG_HEAD_DIM, HG_HEAD_DIM), _F32),
            pltpu.VMEM((PROJ_SLOTS, 4, TILE_ROWS, PAIR_DIM), _F32),
            pltpu.VMEM((HEAD_PAIRS, 4, d, PAIR_DIM), _BF16),
            pltpu.VMEM((HEAD_PAIRS, d // PAIR_DIM, PAIR_DIM, PAIR_DIM), _BF16),
            pltpu.VMEM((STAGE_SLOTS, d, PAIR_DIM), _F32),
            pltpu.VMEM((STAGE_SLOTS, PAIR_DIM, PAIR_DIM), _F32),
            pltpu.SemaphoreType.DMA((STAGE_SLOTS,)),
            pltpu.SemaphoreType.DMA((STAGE_SLOTS,)),
        ],
        compiler_params=pltpu.CompilerParams(
            dimension_semantics=("arbitrary", "arbitrary"), vmem_limit_bytes=VMEM_LIMIT_BYTES),
        name="hgrn2_layer",
    )(x, mod, gain, w, lb_raw, gn_gain, w_out, final_gain)


def kernel(x, c, norm_gain, w_ada, b_ada, a_w_in, a_ln_gain, a_ln_bias, a_w_s, a_b_s, a_w_out,
           b_w_in, b_lower_bounds, b_gn_gain, b_w_out, final_gain):
    bsz, seq, d = x.shape
    depth = norm_gain.shape[0]
    assert depth == 2 and a_w_in.shape[0] == 1 and b_w_in.shape[0] == 1
    assert seq % TOKENS_PER_STEP == 0 and TOKENS_PER_STEP % SG_BLOCK == 0
    assert seq % (HGRN2_TILES_PER_STEP * TILE_ROWS) == 0

    mod = _adaln_mod(c, w_ada, b_ada)
    mod = mod.reshape(depth, bsz, 3, d)

    x1 = _sgu_layer(
        x, mod[0], norm_gain[0:1], a_w_in, a_ln_gain, a_ln_bias, a_w_s[0],
        a_b_s[0].reshape(SG_GROUPS, SG_BLOCK, 1), a_w_out)

    lb_raw = b_lower_bounds.reshape(depth, HEAD_PAIRS, PAIR_DIM).transpose(1, 0, 2)
    return _hgrn2_layer(
        x1, mod[1], norm_gain[1:2], b_w_in, lb_raw, b_gn_gain, b_w_out, final_gain.reshape(1, d),
        layer=1)
```

```python
import functools
import math

import jax
import jax.numpy as jnp
from jax import lax
from jax.experimental import pallas as pl
from jax.experimental.pallas import tpu as pltpu

D_MODEL = 1024
D_INNER = 2048
CHUNK = 64
SG_BLOCK = 128
SG_GROUPS = 8
SG_GROUP_DIM = D_INNER // SG_GROUPS
HG_HEADS = 16
HG_HEAD_DIM = D_INNER // HG_HEADS
HEAD_PAIRS = HG_HEADS // 2
PAIR_DIM = 2 * HG_HEAD_DIM
EPS = 1e-6

TOKENS_PER_STEP = 256
TILE_ROWS = 256
HGRN2_TILES_PER_STEP = 1
PROJ_SLOTS = 2
STAGE_SLOTS = 4
VMEM_LIMIT_BYTES = 56 * 1024 * 1024

_BF16 = jnp.bfloat16
_F32 = jnp.float32


def _dot(a, b):
    return jnp.dot(a, b, preferred_element_type=_F32)


def _sigmoid(z):
    return 1.0 / (1.0 + jnp.exp(-z))


def _silu(z):
    return z * _sigmoid(z)


def _gelu_tanh(z):
    c = math.sqrt(2.0 / math.pi)
    return 0.5 * z * (1.0 + jnp.tanh(c * (z + 0.044715 * (z * z * z))))


def _modulated_norm(x, gain, mod_ref):
    r = lax.rsqrt(jnp.mean(x * x, axis=-1, keepdims=True) + EPS)
    shift = mod_ref[0, 0:1, :]
    scale = mod_ref[0, 1:2, :]
    return (x * r) * (gain * (1.0 + scale)) + shift


def _stage_weight_blocks(src_hbm, dst_ref, stage_ref, sem_ref, blocks):
    slots = stage_ref.shape[0]

    def block_copy(j):
        return pltpu.make_async_copy(
            src_hbm.at[blocks[j][0]], stage_ref.at[j % slots], sem_ref.at[j % slots])

    for j in range(min(slots - 1, len(blocks))):
        block_copy(j).start()
    for j in range(len(blocks)):
        if j + slots - 1 < len(blocks):
            block_copy(j + slots - 1).start()
        block_copy(j).wait()
        dst_ref[blocks[j][1]] = stage_ref[j % slots].astype(_BF16)


def _first_grid_step():
    return (pl.program_id(0) == 0) & (pl.program_id(1) == 0)


def _adaln_kernel(c_ref, w_ref, b_ref, o_ref):
    c_act = _silu(c_ref[...]).astype(_BF16)
    o_ref[0] = _dot(c_act, w_ref[0].astype(_BF16)) + b_ref[0]


def _adaln_mod(c, w_ada, b_ada):
    depth, d, n3 = w_ada.shape
    bsz = c.shape[0]
    nt = n3 // d
    return pl.pallas_call(
        _adaln_kernel,
        grid=(depth, nt),
        in_specs=[
            pl.BlockSpec((bsz, d), lambda l, j: (0, 0)),
            pl.BlockSpec((1, d, d), lambda l, j: (l, 0, j)),
            pl.BlockSpec((1, 1, d), lambda l, j: (l, 0, j)),
        ],
        out_specs=pl.BlockSpec((1, bsz, d), lambda l, j: (l, 0, j)),
        out_shape=jax.ShapeDtypeStruct((depth, bsz, n3), _F32),
        compiler_params=pltpu.CompilerParams(dimension_semantics=("arbitrary", "arbitrary")),
        name="adaln_mod",
    )(c, w_ada, b_ada.reshape(depth, 1, n3))


def _sgu_kernel(x_ref, mod_ref, gain_ref, win_hbm, lng_ref, lnb_ref, ws_ref, bs_ref, wout_hbm,
                o_ref, h_ref, v_ref, ug_ref, acc_ref, win_ref, wout_ref, win_stage, wout_stage, win_sems, wout_sems):
    tm = x_ref.shape[1]
    U, V, GATE = range(3)

    @pl.when(_first_grid_step())
    def _():
        _stage_weight_blocks(
            win_hbm, win_ref, win_stage, win_sems,
            [((0, pl.ds(0, D_MODEL), pl.ds(p * D_INNER + g * SG_GROUP_DIM, SG_GROUP_DIM)), (p, g))
             for p in range(3) for g in range(SG_GROUPS)])
        _stage_weight_blocks(
            wout_hbm, wout_ref, wout_stage, wout_sems,
            [((0, pl.ds(g * SG_GROUP_DIM, SG_GROUP_DIM), pl.ds(0, D_MODEL)), (g,)) for g in range(SG_GROUPS)])

    x = x_ref[0]
    h_ref[...] = _modulated_norm(x, gain_ref[...], mod_ref).astype(_BF16)

    t_chunk = lax.broadcasted_iota(jnp.int32, (SG_BLOCK, SG_BLOCK), 0) // CHUNK
    s_chunk = lax.broadcasted_iota(jnp.int32, (SG_BLOCK, SG_BLOCK), 1) // CHUNK
    allowed = s_chunk <= t_chunk

    def group_cols(g):
        return slice(g * SG_GROUP_DIM, (g + 1) * SG_GROUP_DIM)

    def project_ug(g):
        ug_ref[g % 2, 0] = _dot(h_ref[...], win_ref[U, g])
        ug_ref[g % 2, 1] = _dot(h_ref[...], win_ref[GATE, g])

    def lane_halves_sum(z):
        return z[:, :SG_GROUP_DIM // 2] + z[:, SG_GROUP_DIM // 2:]

    def out_project(g, y):
        contrib = _dot(y, wout_ref[g])
        if g == 0:
            acc_ref[...] = contrib
        else:
            acc_ref[...] += contrib

    row_sum = None
    for g in range(SG_GROUPS):
        vg = _gelu_tanh(_dot(h_ref[...], win_ref[V, g]))
        v_ref[g] = vg
        row_sum = lane_halves_sum(vg) if row_sum is None else row_sum + lane_halves_sum(vg)
    mu = jnp.sum(row_sum, axis=-1, keepdims=True) * (1.0 / D_INNER)
    project_ug(0)
    sq_sum = None
    for g in range(SG_GROUPS):
        vc = v_ref[g] - mu
        sq_sum = lane_halves_sum(vc * vc) if sq_sum is None else sq_sum + lane_halves_sum(vc * vc)
    var = jnp.sum(sq_sum, axis=-1, keepdims=True) * (1.0 / D_INNER)
    rstd = lax.rsqrt(var + EPS)

    for g in range(SG_GROUPS):
        vn = ((v_ref[g] - mu) * rstd * lng_ref[:, group_cols(g)] + lnb_ref[:, group_cols(g)]).astype(_BF16)
        ws = jnp.where(allowed, ws_ref[g], 0.0).astype(_BF16)
        bias = bs_ref[g]
        s = jnp.concatenate(
            [_dot(ws, vn[blk * SG_BLOCK:(blk + 1) * SG_BLOCK, :]) + bias for blk in range(tm // SG_BLOCK)],
            axis=0)
        if g + 1 < SG_GROUPS:
            project_ug(g + 1)
        if g > 0:
            out_project(g - 1, y_prev)
        u = _gelu_tanh(ug_ref[g % 2, 0])
        gate = _silu(ug_ref[g % 2, 1])
        y_prev = (u * s * gate).astype(_BF16)
    out_project(SG_GROUPS - 1, y_prev)

    o_ref[0] = x + mod_ref[0, 2:3, :] * acc_ref[...]


def _sgu_layer(x, mod, gain, w_in, ln_gain, ln_bias, w_s, b_s, w_out):
    bsz, seq, d = x.shape
    tm = TOKENS_PER_STEP
    const = lambda nd: (lambda b, t: (0,) * nd)
    resident = functools.partial(pl.BlockSpec, pipeline_mode=pl.Buffered(1))
    in_hbm = pl.BlockSpec(memory_space=pl.ANY)
    return pl.pallas_call(
        _sgu_kernel,
        grid=(bsz, seq // tm),
        in_specs=[
            pl.BlockSpec((1, tm, d), lambda b, t: (b, t, 0)),
            pl.BlockSpec((1, 3, d), lambda b, t: (b, 0, 0)),
            resident((1, d), const(2)),
            in_hbm,
            resident((1, D_INNER), const(2)),
            resident((1, D_INNER), const(2)),
            resident(w_s.shape, const(3)),
            resident(b_s.shape, const(3)),
            in_hbm,
        ],
        out_specs=pl.BlockSpec((1, tm, d), lambda b, t: (b, t, 0)),
        out_shape=jax.ShapeDtypeStruct(x.shape, x.dtype),
        scratch_shapes=[
            pltpu.VMEM((tm, d), _BF16),
            pltpu.VMEM((SG_GROUPS, tm, SG_GROUP_DIM), _F32),
            pltpu.VMEM((2, 2, tm, SG_GROUP_DIM), _F32),
            pltpu.VMEM((tm, d), _F32),
            pltpu.VMEM((3, SG_GROUPS, d, SG_GROUP_DIM), _BF16),
            pltpu.VMEM((SG_GROUPS, SG_GROUP_DIM, d), _BF16),
            pltpu.VMEM((STAGE_SLOTS, d, SG_GROUP_DIM), _F32),
            pltpu.VMEM((STAGE_SLOTS, SG_GROUP_DIM, d), _F32),
            pltpu.SemaphoreType.DMA((STAGE_SLOTS,)),
            pltpu.SemaphoreType.DMA((STAGE_SLOTS,)),
        ],
        compiler_params=pltpu.CompilerParams(
            dimension_semantics=("arbitrary", "arbitrary"), vmem_limit_bytes=VMEM_LIMIT_BYTES),
        name="sgu_layer",
    )(x, mod, gain, w_in, ln_gain, ln_bias, w_s, b_s, w_out)


def _hgrn2_kernel(x_ref, mod_ref, gain_ref, w_hbm, lbraw_ref, gn_ref, wout_hbm, fgain_ref,
                  o_ref, h_ref, acc_ref, state_ref, proj_ref, w_ref, wout_ref, w_stage, wout_stage, w_sems, wout_sems,
                  *, layer):
    tm = TILE_ROWS
    n_tiles = x_ref.shape[1] // tm
    n_items = n_tiles * HEAD_PAIRS
    n_chunks = tm // CHUNK
    Q, F, I, G = range(4)

    @pl.when(_first_grid_step())
    def _():
        _stage_weight_blocks(
            w_hbm, w_ref, w_stage, w_sems,
            [((0, pl.ds(0, D_MODEL), pl.ds(part * D_INNER + hp * PAIR_DIM, PAIR_DIM)), (hp, part))
             for part in range(4) for hp in range(HEAD_PAIRS)])
        _stage_weight_blocks(
            wout_hbm, wout_ref, wout_stage, wout_sems,
            [((0, pl.ds(hp * PAIR_DIM, PAIR_DIM), pl.ds(n * PAIR_DIM, PAIR_DIM)), (hp, n))
             for hp in range(HEAD_PAIRS) for n in range(D_MODEL // PAIR_DIM)])

    @pl.when(pl.program_id(1) == 0)
    def _():
        state_ref[...] = jnp.zeros_like(state_ref)

    row = lax.broadcasted_iota(jnp.int32, (tm, tm), 0)
    col = lax.broadcasted_iota(jnp.int32, (tm, tm), 1)
    intra = (row // CHUNK == col // CHUNK) & (col <= row)
    step_in_chunk = lax.broadcasted_iota(jnp.int32, (tm, PAIR_DIM), 0) % CHUNK
    gn = gn_ref[...]
    heads = [slice(hd * HG_HEAD_DIM, (hd + 1) * HG_HEAD_DIM) for hd in range(2)]
    chunks = [slice(c * CHUNK, (c + 1) * CHUNK) for c in range(n_chunks)]
    row_halves = [slice(0, tm // 2), slice(tm // 2, tm)]

    def tile_rows(tile):
        return slice(tile * tm, (tile + 1) * tm)

    def normalize_tile(tile):
        h_ref[tile] = _modulated_norm(x_ref[0, tile_rows(tile)], gain_ref[...], mod_ref).astype(_BF16)

    def finish_tile(tile):
        xo = x_ref[0, tile_rows(tile)] + mod_ref[0, 2:3, :] * acc_ref[tile]
        r = lax.rsqrt(jnp.mean(xo * xo, axis=-1, keepdims=True) + EPS)
        o_ref[0, tile_rows(tile)] = xo * r * fgain_ref[...]

    def project(it, part):
        tile, hp = divmod(it, HEAD_PAIRS)
        for rows in row_halves:
            proj_ref[it % PROJ_SLOTS, part, rows] = _dot(h_ref[tile, rows], w_ref[hp, part])

    def forget_gate(it):
        hp = it % HEAD_PAIRS
        lbraw = lbraw_ref[hp]
        e = jnp.exp(lbraw - jnp.max(lbraw, axis=0, keepdims=True))
        p = e / jnp.sum(e, axis=0, keepdims=True)
        lb = jnp.sum(p[1:layer + 1], axis=0, keepdims=True)
        f = lb + (1.0 - lb) * _sigmoid(proj_ref[it % PROJ_SLOTS, F])
        return 1.0 - f, jnp.log(f)

    def prefix_sum(z):
        shift = 1
        while shift < CHUNK:
            z = z + jnp.where(step_in_chunk >= shift, pltpu.roll(z, shift, axis=0), 0.0)
            shift *= 2
        return z

    def decay_scaling(it, k, a):
        def chunk_rows(offset):
            return jnp.concatenate(
                [jnp.broadcast_to(a[c * CHUNK + offset:c * CHUNK + offset + 1, :], (CHUNK, PAIR_DIM))
                 for c in range(n_chunks)], axis=0)
        a_mid = chunk_rows(CHUNK // 2 - 1)
        a_end = chunk_rows(CHUNK - 1)
        q = _silu(proj_ref[it % PROJ_SLOTS, Q])
        q_in = q * jnp.exp(a - a_mid)
        k_in = k * jnp.exp(a_mid - a)
        q_out = (q_in * jnp.exp(a_mid)).astype(_BF16)
        k_out = (k_in * jnp.exp(a_end - a_mid)).astype(_BF16)
        decay = [jnp.exp(a[c * CHUNK + CHUNK - 1:c * CHUNK + CHUNK, :]) for c in range(n_chunks)]
        k_in_t = k_in.T.astype(_BF16)
        return q_in.astype(_BF16), k_in_t, q_out, k_out, decay

    def scores_and_updates(it, q_in, k_in_t, k_out):
        vv32 = proj_ref[it % PROJ_SLOTS, I]
        vv = vv32.astype(_BF16)
        vv_t = vv32.T.astype(_BF16)
        scores = [_dot(q_in[:, ln], k_in_t[ln, :]) for ln in heads]
        kv = []
        for ln in heads:
            blocks = []
            for c, rw in enumerate(chunks):
                pieces = [k_out[rw, ln]]
                if c > 0:
                    pieces.insert(0, jnp.zeros((c * CHUNK, HG_HEAD_DIM), _BF16))
                if c + 1 < n_chunks:
                    pieces.append(jnp.zeros(((n_chunks - 1 - c) * CHUNK, HG_HEAD_DIM), _BF16))
                blocks.append(jnp.concatenate(pieces, axis=0))
            kv.append(_dot(vv_t[ln, :], jnp.concatenate(blocks, axis=1)))
        return vv, scores, kv

    def mask_and_states(it, scores, kv, decay):
        hp = it % HEAD_PAIRS
        scores = [jnp.where(intra, s, 0.0).astype(_BF16) for s in scores]
        states = []
        for hd, ln in enumerate(heads):
            st = state_ref[hp * 2 + hd]
            per_chunk = []
            for c in range(n_chunks):
                per_chunk.append(st.T.astype(_BF16))
                st = decay[c][:, ln] * st + kv[hd][:, c * HG_HEAD_DIM:(c + 1) * HG_HEAD_DIM]
            state_ref[hp * 2 + hd] = st
            states.append(per_chunk)
        return scores, states

    def outputs(vv, scores, q_out, states):
        o_heads = []
        for hd, ln in enumerate(heads):
            o_chunks = []
            for c, rw in enumerate(chunks):
                blk = slice((c // 2) * SG_BLOCK, (c // 2 + 1) * SG_BLOCK)
                lhs = jnp.concatenate([q_out[rw, ln], scores[hd][rw, blk]], axis=1)
                rhs = jnp.concatenate([states[hd][c], vv[blk, ln]], axis=0)
                o_chunks.append(_dot(lhs, rhs))
            o_heads.append(jnp.concatenate(o_chunks, axis=0))
        return o_heads

    def gated_norm(it, o_heads):
        og = _silu(proj_ref[it % PROJ_SLOTS, G])
        normed = [o * lax.rsqrt(jnp.mean(o * o, axis=-1, keepdims=True) + EPS) * gn for o in o_heads]
        return (jnp.concatenate(normed, axis=1) * og).astype(_BF16)

    def out_project(it, y, blocks=range(D_MODEL // PAIR_DIM)):
        tile, hp = divmod(it, HEAD_PAIRS)
        for n in blocks:
            cols = slice(n * PAIR_DIM, (n + 1) * PAIR_DIM)
            contrib = _dot(y, wout_ref[hp, n])
            if hp == 0:
                acc_ref[tile, :, cols] = contrib
            else:
                acc_ref[tile, :, cols] += contrib
        if hp == HEAD_PAIRS - 1 and blocks[-1] == D_MODEL // PAIR_DIM - 1:
            finish_tile(tile)

    normalize_tile(0)
    for part in (F, Q, I, G):
        project(0, part)
    y_prev = None
    for it in range(n_items):
        tile, hp = divmod(it, HEAD_PAIRS)
        nxt = it + 1 if it + 1 < n_items else None
        if hp == HEAD_PAIRS // 2 and tile + 1 < n_tiles:
            normalize_tile(tile + 1)
        k, log_f = forget_gate(it)
        a = prefix_sum(log_f)
        if y_prev is not None:
            out_project(it - 1, y_prev, range(0, 2))
        if nxt is not None:
            project(nxt, F)
        q_in, k_in_t, q_out, k_out, decay = decay_scaling(it, k, a)
        vv, scores, kv = scores_and_updates(it, q_in, k_in_t, k_out)
        if y_prev is not None:
            out_project(it - 1, y_prev, range(2, 4))
        if nxt is not None:
            project(nxt, Q)
        scores, states = mask_and_states(it, scores, kv, decay)
        o_heads = outputs(vv, scores, q_out, states)
        if nxt is not None:
            project(nxt, I)
            project(nxt, G)
        y_prev = gated_norm(it, o_heads)
    out_project(n_items - 1, y_prev)


def _hgrn2_layer(x, mod, gain, w, lb_raw, gn_gain, w_out, final_gain, layer):
    bsz, seq, d = x.shape
    tm = HGRN2_TILES_PER_STEP * TILE_ROWS
    const = lambda nd: (lambda b, t: (0,) * nd)
    resident = functools.partial(pl.BlockSpec, pipeline_mode=pl.Buffered(1))
    in_hbm = pl.BlockSpec(memory_space=pl.ANY)
    return pl.pallas_call(
        functools.partial(_hgrn2_kernel, layer=layer),
        grid=(bsz, seq // tm),
        in_specs=[
            pl.BlockSpec((1, tm, d), lambda b, t: (b, t, 0)),
            pl.BlockSpec((1, 3, d), lambda b, t: (b, 0, 0)),
            resident((1, d), const(2)),
            in_hbm,
            resident(lb_raw.shape, const(3)),
            resident((1, HG_HEAD_DIM), const(2)),
            in_hbm,
            resident((1, d), const(2)),
        ],
        out_specs=pl.BlockSpec((1, tm, d), lambda b, t: (b, t, 0)),
        out_shape=jax.ShapeDtypeStruct(x.shape, x.dtype),
        scratch_shapes=[
            pltpu.VMEM((HGRN2_TILES_PER_STEP, TILE_ROWS, d), _BF16),
            pltpu.VMEM((HGRN2_TILES_PER_STEP, TILE_ROWS, d), _F32),
            pltpu.VMEM((HG_HEADS, HG_HEAD_DIM, HG_HEAD_DIM), _F32),
            pltpu.VMEM((PROJ_SLOTS, 4, TILE_ROWS, PAIR_DIM), _F32),
            pltpu.VMEM((HEAD_PAIRS, 4, d, PAIR_DIM), _BF16),
            pltpu.VMEM((HEAD_PAIRS, d // PAIR_DIM, PAIR_DIM, PAIR_DIM), _BF16),
            pltpu.VMEM((STAGE_SLOTS, d, PAIR_DIM), _F32),
            pltpu.VMEM((STAGE_SLOTS, PAIR_DIM, PAIR_DIM), _F32),
            pltpu.SemaphoreType.DMA((STAGE_SLOTS,)),
            pltpu.SemaphoreType.DMA((STAGE_SLOTS,)),
        ],
        compiler_params=pltpu.CompilerParams(
            dimension_semantics=("arbitrary", "arbitrary"), vmem_limit_bytes=VMEM_LIMIT_BYTES),
        name="hgrn2_layer",
    )(x, mod, gain, w, lb_raw, gn_gain, w_out, final_gain)


def kernel(x, c, norm_gain, w_ada, b_ada, a_w_in, a_ln_gain, a_ln_bias, a_w_s, a_b_s, a_w_out,
           b_w_in, b_lower_bounds, b_gn_gain, b_w_out, final_gain):
    bsz, seq, d = x.shape
    depth = norm_gain.shape[0]
    assert depth == 2 and a_w_in.shape[0] == 1 and b_w_in.shape[0] == 1
    assert seq % TOKENS_PER_STEP == 0 and TOKENS_PER_STEP % SG_BLOCK == 0
    assert seq % (HGRN2_TILES_PER_STEP * TILE_ROWS) == 0

    mod = _adaln_mod(c, w_ada, b_ada)
    mod = mod.reshape(depth, bsz, 3, d)

    x1 = _sgu_layer(
        x, mod[0], norm_gain[0:1], a_w_in, a_ln_gain, a_ln_bias, a_w_s[0],
        a_b_s[0].reshape(SG_GROUPS, SG_BLOCK, 1), a_w_out)

    lb_raw = b_lower_bounds.reshape(depth, HEAD_PAIRS, PAIR_DIM).transpose(1, 0, 2)
    return _hgrn2_layer(
        x1, mod[1], norm_gain[1:2], b_w_in, lb_raw, b_gn_gain, b_w_out, final_gain.reshape(1, d),
        layer=1)
```

```python
import functools
import math

import jax
import jax.numpy as jnp
from jax import lax
from jax.experimental import pallas as pl
from jax.experimental.pallas import tpu as pltpu

D_MODEL = 1024
D_INNER = 2048
CHUNK = 64
SG_BLOCK = 128
SG_GROUPS = 8
SG_GROUP_DIM = D_INNER // SG_GROUPS
HG_HEADS = 16
HG_HEAD_DIM = D_INNER // HG_HEADS
HEAD_PAIRS = HG_HEADS // 2
PAIR_DIM = 2 * HG_HEAD_DIM
EPS = 1e-6

TOKENS_PER_STEP = 256
TILE_ROWS = 256
HGRN2_TILES_PER_STEP = 1
PROJ_SLOTS = 2
STAGE_SLOTS = 4
VMEM_LIMIT_BYTES = 56 * 1024 * 1024

_BF16 = jnp.bfloat16
_F32 = jnp.float32


def _dot(a, b):
    return jnp.dot(a, b, preferred_element_type=_F32)


def _sigmoid(z):
    return 1.0 / (1.0 + jnp.exp(-z))


def _silu(z):
    return z * _sigmoid(z)


def _gelu_tanh(z):
    c = math.sqrt(2.0 / math.pi)
    return 0.5 * z * (1.0 + jnp.tanh(c * (z + 0.044715 * (z * z * z))))


def _modulated_norm(x, gain, mod_ref):
    r = lax.rsqrt(jnp.mean(x * x, axis=-1, keepdims=True) + EPS)
    shift = mod_ref[0, 0:1, :]
    scale = mod_ref[0, 1:2, :]
    return (x * r) * (gain * (1.0 + scale)) + shift


def _stage_weight_blocks(src_hbm, dst_ref, stage_ref, sem_ref, blocks):
    slots = stage_ref.shape[0]

    def block_copy(j):
        return pltpu.make_async_copy(
            src_hbm.at[blocks[j][0]], stage_ref.at[j % slots], sem_ref.at[j % slots])

    for j in range(min(slots - 1, len(blocks))):
        block_copy(j).start()
    for j in range(len(blocks)):
        if j + slots - 1 < len(blocks):
            block_copy(j + slots - 1).start()
        block_copy(j).wait()
        dst_ref[blocks[j][1]] = stage_ref[j % slots].astype(_BF16)


def _first_grid_step():
    return (pl.program_id(0) == 0) & (pl.program_id(1) == 0)


def _adaln_kernel(c_ref, w_ref, b_ref, o_ref):
    c_act = _silu(c_ref[...]).astype(_BF16)
    o_ref[0] = _dot(c_act, w_ref[0].astype(_BF16)) + b_ref[0]


def _adaln_mod(c, w_ada, b_ada):
    depth, d, n3 = w_ada.shape
    bsz = c.shape[0]
    nt = n3 // d
    return pl.pallas_call(
        _adaln_kernel,
        grid=(depth, nt),
        in_specs=[
            pl.BlockSpec((bsz, d), lambda l, j: (0, 0)),
            pl.BlockSpec((1, d, d), lambda l, j: (l, 0, j)),
            pl.BlockSpec((1, 1, d), lambda l, j: (l, 0, j)),
        ],
        out_specs=pl.BlockSpec((1, bsz, d), lambda l, j: (l, 0, j)),
        out_shape=jax.ShapeDtypeStruct((depth, bsz, n3), _F32),
        compiler_params=pltpu.CompilerParams(dimension_semantics=("arbitrary", "arbitrary")),
        name="adaln_mod",
    )(c, w_ada, b_ada.reshape(depth, 1, n3))


def _sgu_kernel(x_ref, mod_ref, gain_ref, win_hbm, lng_ref, lnb_ref, ws_ref, bs_ref, wout_hbm,
                o_ref, h_ref, v_ref, ug_ref, acc_ref, win_ref, wout_ref, win_stage, wout_stage, win_sems, wout_sems):
    tm = x_ref.shape[1]
    U, V, GATE = range(3)

    @pl.when(_first_grid_step())
    def _():
        _stage_weight_blocks(
            win_hbm, win_ref, win_stage, win_sems,
            [((0, pl.ds(0, D_MODEL), pl.ds(p * D_INNER + g * SG_GROUP_DIM, SG_GROUP_DIM)), (p, g))
             for p in range(3) for g in range(SG_GROUPS)])
        _stage_weight_blocks(
            wout_hbm, wout_ref, wout_stage, wout_sems,
            [((0, pl.ds(g * SG_GROUP_DIM, SG_GROUP_DIM), pl.ds(0, D_MODEL)), (g,)) for g in range(SG_GROUPS)])

    x = x_ref[0]
    h_ref[...] = _modulated_norm(x, gain_ref[...], mod_ref).astype(_BF16)

    t_chunk = lax.broadcasted_iota(jnp.int32, (SG_BLOCK, SG_BLOCK), 0) // CHUNK
    s_chunk = lax.broadcasted_iota(jnp.int32, (SG_BLOCK, SG_BLOCK), 1) // CHUNK
    allowed = s_chunk <= t_chunk

    def group_cols(g):
        return slice(g * SG_GROUP_DIM, (g + 1) * SG_GROUP_DIM)

    def project_ug(g):
        ug_ref[g % 2, 0] = _dot(h_ref[...], win_ref[U, g])
        ug_ref[g % 2, 1] = _dot(h_ref[...], win_ref[GATE, g])

    def lane_halves_sum(z):
        return z[:, :SG_GROUP_DIM // 2] + z[:, SG_GROUP_DIM // 2:]

    def out_project(g, y):
        contrib = _dot(y, wout_ref[g])
        if g == 0:
            acc_ref[...] = contrib
        else:
            acc_ref[...] += contrib

    row_sum = None
    for g in range(SG_GROUPS):
        vg = _gelu_tanh(_dot(h_ref[...], win_ref[V, g]))
        v_ref[g] = vg
        row_sum = lane_halves_sum(vg) if row_sum is None else row_sum + lane_halves_sum(vg)
    mu = jnp.sum(row_sum, axis=-1, keepdims=True) * (1.0 / D_INNER)
    project_ug(0)
    sq_sum = None
    for g in range(SG_GROUPS):
        vc = v_ref[g] - mu
        sq_sum = lane_halves_sum(vc * vc) if sq_sum is None else sq_sum + lane_halves_sum(vc * vc)
    var = jnp.sum(sq_sum, axis=-1, keepdims=True) * (1.0 / D_INNER)
    rstd = lax.rsqrt(var + EPS)

    for g in range(SG_GROUPS):
        vn = ((v_ref[g] - mu) * rstd * lng_ref[:, group_cols(g)] + lnb_ref[:, group_cols(g)]).astype(_BF16)
        ws = jnp.where(allowed, ws_ref[g], 0.0).astype(_BF16)
        bias = bs_ref[g]
        s = jnp.concatenate(
            [_dot(ws, vn[blk * SG_BLOCK:(blk + 1) * SG_BLOCK, :]) + bias for blk in range(tm // SG_BLOCK)],
            axis=0)
        if g + 1 < SG_GROUPS:
            project_ug(g + 1)
        if g > 0:
            out_project(g - 1, y_prev)
        u = _gelu_tanh(ug_ref[g % 2, 0])
        gate = _silu(ug_ref[g % 2, 1])
        y_prev = (u * s * gate).astype(_BF16)
    out_project(SG_GROUPS - 1, y_prev)

    o_ref[0] = x + mod_ref[0, 2:3, :] * acc_ref[...]


def _sgu_layer(x, mod, gain, w_in, ln_gain, ln_bias, w_s, b_s, w_out):
    bsz, seq, d = x.shape
    tm = TOKENS_PER_STEP
    const = lambda nd: (lambda b, t: (0,) * nd)
    resident = functools.partial(pl.BlockSpec, pipeline_mode=pl.Buffered(1))
    in_hbm = pl.BlockSpec(memory_space=pl.ANY)
    return pl.pallas_call(
        _sgu_kernel,
        grid=(bsz, seq // tm),
        in_specs=[
            pl.BlockSpec((1, tm, d), lambda b, t: (b, t, 0)),
            pl.BlockSpec((1, 3, d), lambda b, t: (b, 0, 0)),
            resident((1, d), const(2)),
            in_hbm,
            resident((1, D_INNER), const(2)),
            resident((1, D_INNER), const(2)),
            resident(w_s.shape, const(3)),
            resident(b_s.shape, const(3)),
            in_hbm,
        ],
        out_specs=pl.BlockSpec((1, tm, d), lambda b, t: (b, t, 0)),
        out_shape=jax.ShapeDtypeStruct(x.shape, x.dtype),
        scratch_shapes=[
            pltpu.VMEM((tm, d), _BF16),
            pltpu.VMEM((SG_GROUPS, tm, SG_GROUP_DIM), _F32),
            pltpu.VMEM((2, 2, tm, SG_GROUP_DIM), _F32),
            pltpu.VMEM((tm, d), _F32),
            pltpu.VMEM((3, SG_GROUPS, d, SG_GROUP_DIM), _BF16),
            pltpu.VMEM((SG_GROUPS, SG_GROUP_DIM, d), _BF16),
            pltpu.VMEM((STAGE_SLOTS, d, SG_GROUP_DIM), _F32),
            pltpu.VMEM((STAGE_SLOTS, SG_GROUP_DIM, d), _F32),
            pltpu.SemaphoreType.DMA((STAGE_SLOTS,)),
            pltpu.SemaphoreType.DMA((STAGE_SLOTS,)),
        ],
        compiler_params=pltpu.CompilerParams(
            dimension_semantics=("arbitrary", "arbitrary"), vmem_limit_bytes=VMEM_LIMIT_BYTES),
        name="sgu_layer",
    )(x, mod, gain, w_in, ln_gain, ln_bias, w_s, b_s, w_out)


def _hgrn2_kernel(x_ref, mod_ref, gain_ref, w_hbm, lbraw_ref, gn_ref, wout_hbm, fgain_ref,
                  o_ref, h_ref, acc_ref, state_ref, proj_ref, w_ref, wout_ref, w_stage, wout_stage, w_sems, wout_sems,
                  *, layer):
    tm = TILE_ROWS
    n_tiles = x_ref.shape[1] // tm
    n_items = n_tiles * HEAD_PAIRS
    n_chunks = tm // CHUNK
    Q, F, I, G = range(4)

    @pl.when(_first_grid_step())
    def _():
        _stage_weight_blocks(
            w_hbm, w_ref, w_stage, w_sems,
            [((0, pl.ds(0, D_MODEL), pl.ds(part * D_INNER + hp * PAIR_DIM, PAIR_DIM)), (hp, part))
             for part in range(4) for hp in range(HEAD_PAIRS)])
        _stage_weight_blocks(
            wout_hbm, wout_ref, wout_stage, wout_sems,
            [((0, pl.ds(hp * PAIR_DIM, PAIR_DIM), pl.ds(n * PAIR_DIM, PAIR_DIM)), (hp, n))
             for hp in range(HEAD_PAIRS) for n in range(D_MODEL // PAIR_DIM)])

    @pl.when(pl.program_id(1) == 0)
    def _():
        state_ref[...] = jnp.zeros_like(state_ref)

    row = lax.broadcasted_iota(jnp.int32, (tm, tm), 0)
    col = lax.broadcasted_iota(jnp.int32, (tm, tm), 1)
    intra = (row // CHUNK == col // CHUNK) & (col <= row)
    step_in_chunk = lax.broadcasted_iota(jnp.int32, (tm, PAIR_DIM), 0) % CHUNK
    gn = gn_ref[...]
    heads = [slice(hd * HG_HEAD_DIM, (hd + 1) * HG_HEAD_DIM) for hd in range(2)]
    chunks = [slice(c * CHUNK, (c + 1) * CHUNK) for c in range(n_chunks)]
    row_halves = [slice(0, tm // 2), slice(tm // 2, tm)]

    def tile_rows(tile):
        return slice(tile * tm, (tile + 1) * tm)

    def normalize_tile(tile):
        h_ref[tile] = _modulated_norm(x_ref[0, tile_rows(tile)], gain_ref[...], mod_ref).astype(_BF16)

    def finish_tile(tile):
        xo = x_ref[0, tile_rows(tile)] + mod_ref[0, 2:3, :] * acc_ref[tile]
        r = lax.rsqrt(jnp.mean(xo * xo, axis=-1, keepdims=True) + EPS)
        o_ref[0, tile_rows(tile)] = xo * r * fgain_ref[...]

    def project(it, part):
        tile, hp = divmod(it, HEAD_PAIRS)
        for rows in row_halves:
            proj_ref[it % PROJ_SLOTS, part, rows] = _dot(h_ref[tile, rows], w_ref[hp, part])

    def forget_gate(it):
        hp = it % HEAD_PAIRS
        lbraw = lbraw_ref[hp]
        e = jnp.exp(lbraw - jnp.max(lbraw, axis=0, keepdims=True))
        p = e / jnp.sum(e, axis=0, keepdims=True)
        lb = jnp.sum(p[1:layer + 1], axis=0, keepdims=True)
        f = lb + (1.0 - lb) * _sigmoid(proj_ref[it % PROJ_SLOTS, F])
        return 1.0 - f, jnp.log(f)

    def prefix_sum(z):
        shift = 1
        while shift < CHUNK:
            z = z + jnp.where(step_in_chunk >= shift, pltpu.roll(z, shift, axis=0), 0.0)
            shift *= 2
        return z

    def decay_scaling(it, k, a):
        def chunk_rows(offset):
            return jnp.concatenate(
                [jnp.broadcast_to(a[c * CHUNK + offset:c * CHUNK + offset + 1, :], (CHUNK, PAIR_DIM))
                 for c in range(n_chunks)], axis=0)
        a_mid = chunk_rows(CHUNK // 2 - 1)
        a_end = chunk_rows(CHUNK - 1)
        q = _silu(proj_ref[it % PROJ_SLOTS, Q])
        q_in = q * jnp.exp(a - a_mid)
        k_in = k * jnp.exp(a_mid - a)
        q_out = (q_in * jnp.exp(a_mid)).astype(_BF16)
        k_out = (k_in * jnp.exp(a_end - a_mid)).astype(_BF16)
        decay = [jnp.exp(a[c * CHUNK + CHUNK - 1:c * CHUNK + CHUNK, :]) for c in range(n_chunks)]
        k_in_t = k_in.T.astype(_BF16)
        return q_in.astype(_BF16), k_in_t, q_out, k_out, decay

    def scores_and_updates(it, q_in, k_in_t, k_out):
        vv32 = proj_ref[it % PROJ_SLOTS, I]
        vv = vv32.astype(_BF16)
        vv_t = vv32.T.astype(_BF16)
        scores = [_dot(q_in[:, ln], k_in_t[ln, :]) for ln in heads]
        kv = []
        for ln in heads:
            blocks = []
            for c, rw in enumerate(chunks):
                pieces = [k_out[rw, ln]]
                if c > 0:
                    pieces.insert(0, jnp.zeros((c * CHUNK, HG_HEAD_DIM), _BF16))
                if c + 1 < n_chunks:
                    pieces.append(jnp.zeros(((n_chunks - 1 - c) * CHUNK, HG_HEAD_DIM), _BF16))
                blocks.append(jnp.concatenate(pieces, axis=0))
            kv.append(_dot(vv_t[ln, :], jnp.concatenate(blocks, axis=1)))
        return vv, scores, kv

    def mask_and_states(it, scores, kv, decay):
        hp = it % HEAD_PAIRS
        scores = [jnp.where(intra, s, 0.0).astype(_BF16) for s in scores]
        states = []
        for hd, ln in enumerate(heads):
            st = state_ref[hp * 2 + hd]
            per_chunk = []
            for c in range(n_chunks):
                per_chunk.append(st.T.astype(_BF16))
                st = decay[c][:, ln] * st + kv[hd][:, c * HG_HEAD_DIM:(c + 1) * HG_HEAD_DIM]
            state_ref[hp * 2 + hd] = st
            states.append(per_chunk)
        return scores, states

    def outputs(vv, scores, q_out, states):
        o_heads = []
        for hd, ln in enumerate(heads):
            o_chunks = []
            for c, rw in enumerate(chunks):
                blk = slice((c // 2) * SG_BLOCK, (c // 2 + 1) * SG_BLOCK)
                lhs = jnp.concatenate([q_out[rw, ln], scores[hd][rw, blk]], axis=1)
                rhs = jnp.concatenate([states[hd][c], vv[blk, ln]], axis=0)
                o_chunks.append(_dot(lhs, rhs))
            o_heads.append(jnp.concatenate(o_chunks, axis=0))
        return o_heads

    def gated_norm(it, o_heads):
        og = _silu(proj_ref[it % PROJ_SLOTS, G])
        normed = [o * lax.rsqrt(jnp.mean(o * o, axis=-1, keepdims=True) + EPS) * gn for o in o_heads]
        return (jnp.concatenate(normed, axis=1) * og).astype(_BF16)

    def out_project(it, y, blocks=range(D_MODEL // PAIR_DIM)):
        tile, hp = divmod(it, HEAD_PAIRS)
        for n in blocks:
            cols = slice(n * PAIR_DIM, (n + 1) * PAIR_DIM)
            contrib = _dot(y, wout_ref[hp, n])
            if hp == 0:
                acc_ref[tile, :, cols] = contrib
            else:
                acc_ref[tile, :, cols] += contrib
        if hp == HEAD_PAIRS - 1 and blocks[-1] == D_MODEL // PAIR_DIM - 1:
            finish_tile(tile)

    normalize_tile(0)
    for part in (F, Q, I, G):
        project(0, part)
    y_prev = None
    for it in range(n_items):
        tile, hp = divmod(it, HEAD_PAIRS)
        nxt = it + 1 if it + 1 < n_items else None
        if hp == HEAD_PAIRS // 2 and tile + 1 < n_tiles:
            normalize_tile(tile + 1)
        k, log_f = forget_gate(it)
        a = prefix_sum(log_f)
        if y_prev is not None:
            out_project(it - 1, y_prev, range(0, 2))
        if nxt is not None:
            project(nxt, F)
        q_in, k_in_t, q_out, k_out, decay = decay_scaling(it, k, a)
        vv, scores, kv = scores_and_updates(it, q_in, k_in_t, k_out)
        if y_prev is not None:
            out_project(it - 1, y_prev, range(2, 3))
        if nxt is not None:
            project(nxt, Q)
        scores, states = mask_and_states(it, scores, kv, decay)
        o_heads = outputs(vv, scores, q_out, states)
        if y_prev is not None:
            out_project(it - 1, y_prev, range(3, 4))
        if nxt is not None:
            project(nxt, I)
            project(nxt, G)
        y_prev = gated_norm(it, o_heads)
    out_project(n_items - 1, y_prev)


def _hgrn2_layer(x, mod, gain, w, lb_raw, gn_gain, w_out, final_gain, layer):
    bsz, seq, d = x.shape
    tm = HGRN2_TILES_PER_STEP * TILE_ROWS
    const = lambda nd: (lambda b, t: (0,) * nd)
    resident = functools.partial(pl.BlockSpec, pipeline_mode=pl.Buffered(1))
    in_hbm = pl.BlockSpec(memory_space=pl.ANY)
    return pl.pallas_call(
        functools.partial(_hgrn2_kernel, layer=layer),
        grid=(bsz, seq // tm),
        in_specs=[
            pl.BlockSpec((1, tm, d), lambda b, t: (b, t, 0)),
            pl.BlockSpec((1, 3, d), lambda b, t: (b, 0, 0)),
            resident((1, d), const(2)),
            in_hbm,
            resident(lb_raw.shape, const(3)),
            resident((1, HG_HEAD_DIM), const(2)),
            in_hbm,
            resident((1, d), const(2)),
        ],
        out_specs=pl.BlockSpec((1, tm, d), lambda b, t: (b, t, 0)),
        out_shape=jax.ShapeDtypeStruct(x.shape, x.dtype),
        scratch_shapes=[
            pltpu.VMEM((HGRN2_TILES_PER_STEP, TILE_ROWS, d), _BF16),
            pltpu.VMEM((HGRN2_TILES_PER_STEP, TILE_ROWS, d), _F32),
            pltpu.VMEM((HG_HEADS, HG_HEAD_DIM, HG_HEAD_DIM), _F32),
            pltpu.VMEM((PROJ_SLOTS, 4, TILE_ROWS, PAIR_DIM), _F32),
            pltpu.VMEM((HEAD_PAIRS, 4, d, PAIR_DIM), _BF16),
            pltpu.VMEM((HEAD_PAIRS, d // PAIR_DIM, PAIR_DIM, PAIR_DIM), _BF16),
            pltpu.VMEM((STAGE_SLOTS, d, PAIR_DIM), _F32),
            pltpu.VMEM((STAGE_SLOTS, PAIR_DIM, PAIR_DIM), _F32),
            pltpu.SemaphoreType.DMA((STAGE_SLOTS,)),
            pltpu.SemaphoreType.DMA((STAGE_SLOTS,)),
        ],
        compiler_params=pltpu.CompilerParams(
            dimension_semantics=("arbitrary", "arbitrary"), vmem_limit_bytes=VMEM_LIMIT_BYTES),
        name="hgrn2_layer",
    )(x, mod, gain, w, lb_raw, gn_gain, w_out, final_gain)


def kernel(x, c, norm_gain, w_ada, b_ada, a_w_in, a_ln_gain, a_ln_bias, a_w_s, a_b_s, a_w_out,
           b_w_in, b_lower_bounds, b_gn_gain, b_w_out, final_gain):
    bsz, seq, d = x.shape
    depth = norm_gain.shape[0]
    assert depth == 2 and a_w_in.shape[0] == 1 and b_w_in.shape[0] == 1
    assert seq % TOKENS_PER_STEP == 0 and TOKENS_PER_STEP % SG_BLOCK == 0
    assert seq % (HGRN2_TILES_PER_STEP * TILE_ROWS) == 0

    mod = _adaln_mod(c, w_ada, b_ada)
    mod = mod.reshape(depth, bsz, 3, d)

    x1 = _sgu_layer(
        x, mod[0], norm_gain[0:1], a_w_in, a_ln_gain, a_ln_bias, a_w_s[0],
        a_b_s[0].reshape(SG_GROUPS, SG_BLOCK, 1), a_w_out)

    lb_raw = b_lower_bounds.reshape(depth, HEAD_PAIRS, PAIR_DIM).transpose(1, 0, 2)
    return _hgrn2_layer(
        x1, mod[1], norm_gain[1:2], b_w_in, lb_raw, b_gn_gain, b_w_out, final_gain.reshape(1, d),
        layer=1)
```
